```python
import numpy as np
import jax
import jax.numpy as jnp
from jax import lax

D_MODEL = 1024
BATCH = 32
SEQ = 2048
DEPTH = 1

POOL_WINDOWS = (2, 4, 8, 16)
POOL_GROUPS = len(POOL_WINDOWS)
POOL_GROUP_W = 128
POOL_W = POOL_GROUPS * POOL_GROUP_W
N_HEADS = 8
HEAD_DIM = 64
N_KV_GROUPS = 2
HEADS_PER_GROUP = N_HEADS // N_KV_GROUPS
ATTN_W = N_HEADS * HEAD_DIM
KV_W = N_KV_GROUPS * HEAD_DIM
N_KV_SLOTS = 6
N_NSA_BRANCHES = 3
CMP_BLOCK = 32
CMP_STRIDE = 16
CMP_HIDDEN = 128
SEL_BLOCK = 64
N_SELECT = 16
WINDOW = 512
QUERY_CHUNK = 16
ROPE_THETA = 500000.0
ROT_DIM = HEAD_DIM // 4
N_MERGE = 2
D_FF = 2816
CONV_WIDTH = 3
IN_W = POOL_W + ATTN_W + N_KV_SLOTS * KV_W + N_HEADS * N_NSA_BRANCHES + N_MERGE * D_MODEL
RMS_EPS = 1e-6
NEG_INF = -1e30
SEL_FORCE = 1e9

kernel_name = "hybrid_pool_nsa_convffn"


def rms_norm(x, g):
    xf = x.astype(jnp.float32)
    y = xf * lax.rsqrt(jnp.mean(xf * xf, axis=-1, keepdims=True) + RMS_EPS)
    return (y * g.astype(jnp.float32)).astype(x.dtype)


def partial_rope(x, pos):
    half = ROT_DIM // 2
    inv = ROPE_THETA ** (-(jnp.arange(half, dtype=jnp.float32) * 2.0 / ROT_DIM))
    ang = pos.astype(jnp.float32)[:, :, None, None] * inv
    cos, sin = jnp.cos(ang), jnp.sin(ang)
    xf = x.astype(jnp.float32)
    x1, x2 = xf[..., :half], xf[..., half:ROT_DIM]
    rot = jnp.concatenate([x1 * cos - x2 * sin, x2 * cos + x1 * sin], axis=-1).astype(x.dtype)
    return jnp.concatenate([rot, x[..., ROT_DIM:]], axis=-1)


def masked_softmax(s, mask, axis):
    s = jnp.where(mask, s.astype(jnp.float32), NEG_INF)
    p = jax.nn.softmax(s, axis=axis)
    return jnp.where(mask, p, 0.0)


def pooling_mixer(u, pool_w, pool_scale):
    S = u.shape[1]
    uf = u.astype(jnp.float32)
    cs = jnp.cumsum(uf, axis=1)
    count = jnp.arange(S, dtype=jnp.float32) + 1.0
    outs = []
    for gi, w in enumerate(POOL_WINDOWS):
        sl = slice(gi * POOL_GROUP_W, (gi + 1) * POOL_GROUP_W)
        c = cs[..., sl]
        lag = jnp.pad(c[:, :S - w], ((0, 0), (w, 0), (0, 0)))
        mean = (c - lag) / jnp.minimum(count, float(w))[None, :, None]
        pooled = (mean - uf[..., sl]).astype(u.dtype)
        outs.append(pooled @ pool_w[gi])
    return jnp.concatenate(outs, axis=-1) * pool_scale


def compress_blocks(kv, pe, w1, w2, idx):
    blk = kv[:, idx] + pe[None, None, :, None, :]
    hid = jax.nn.silu(jnp.einsum('bnlgd,ldh->bngh', blk, w1))
    return jnp.einsum('bngh,hd->bngd', hid, w2)


def nsa_mixer(q, k_cmp, v_cmp, k_sel, v_sel, k_win, v_win, gates, positions,
              q_norm_g, k_norm_g, cmp_pe, cmp_w1, cmp_w2):
    B, S = q.shape[0], q.shape[1]
    G, R, dh = N_KV_GROUPS, HEADS_PER_GROUP, HEAD_DIM
    scale = HEAD_DIM ** -0.5
    q = partial_rope(rms_norm(q, q_norm_g), positions)

    n_cmp = (S - CMP_BLOCK) // CMP_STRIDE + 1
    cmp_idx = np.arange(n_cmp)[:, None] * CMP_STRIDE + np.arange(CMP_BLOCK)[None, :]
    cmp_end = jnp.asarray(cmp_idx[:, -1], dtype=jnp.int32)
    kc = compress_blocks(k_cmp, cmp_pe[0], cmp_w1[0], cmp_w2[0], cmp_idx)
    vc = compress_blocks(v_cmp, cmp_pe[1], cmp_w1[1], cmp_w2[1], cmp_idx)
    kc = partial_rope(rms_norm(kc, k_norm_g[0]), positions[:, cmp_idx[:, -1]])

    n_sel = S // SEL_BLOCK
    k_top = min(N_SELECT, n_sel)
    ks = partial_rope(rms_norm(k_sel, k_norm_g[1]), positions)
    ks_blocks = ks.reshape(B, n_sel, SEL_BLOCK, G, dh).transpose(0, 3, 1, 2, 4)
    vs_blocks = v_sel.reshape(B, n_sel, SEL_BLOCK, G, dh).transpose(0, 3, 1, 2, 4)
    sel_start = np.arange(n_sel)[:, None] * SEL_BLOCK
    cmp_start = np.arange(n_cmp)[None, :] * CMP_STRIDE
    ov = np.clip(np.minimum(sel_start + SEL_BLOCK, cmp_start + CMP_BLOCK)
                 - np.maximum(sel_start, cmp_start), 0, None) / CMP_BLOCK
    overlap = jnp.asarray(ov, dtype=jnp.float32)
    gather_blocks = jax.vmap(jax.vmap(lambda blocks, i: blocks[i]))

    kw = partial_rope(rms_norm(k_win, k_norm_g[2]), positions)
    kw_pad = jnp.pad(kw, ((0, 0), (WINDOW, 0), (0, 0), (0, 0)))
    vw_pad = jnp.pad(v_win, ((0, 0), (WINDOW, 0), (0, 0), (0, 0)))

    n_chunks = S // QUERY_CHUNK
    q_ch = q.reshape(B, n_chunks, QUERY_CHUNK, G, R, dh).swapaxes(0, 1)
    g_ch = gates.reshape(B, n_chunks, QUERY_CHUNK, G, R, N_NSA_BRANCHES).swapaxes(0, 1)
    j_blk = jnp.arange(n_sel)
    tok_off = jnp.arange(SEL_BLOCK)
    win_off = jnp.arange(WINDOW + QUERY_CHUNK)

    def chunk(args):
        c, qc, gc = args
        t = c * QUERY_CHUNK + jnp.arange(QUERY_CHUNK)
        s = jnp.einsum('bqgrd,bngd->bgrqn', qc, kc) * scale
        p_cmp = masked_softmax(s, cmp_end[None, :] <= t[:, None], -1)
        o_cmp = jnp.einsum('bgrqn,bngd->bqgrd', p_cmp.astype(vc.dtype), vc)
        imp = jnp.einsum('bgrqn,jn->bgqj', p_cmp, overlap)
        cur = t // SEL_BLOCK
        forced = (j_blk[None, :] == 0) | (j_blk[None, :] == cur[:, None]) | (j_blk[None, :] == cur[:, None] - 1)
        future = j_blk[None, :] * SEL_BLOCK > t[:, None]
        imp = jnp.where(forced, SEL_FORCE, jnp.where(future, NEG_INF, imp))
        _, top = lax.top_k(imp, k_top)
        ksel = gather_blocks(ks_blocks, top)
        vsel = gather_blocks(vs_blocks, top)
        tok = top[..., None] * SEL_BLOCK + tok_off
        m_sel = (tok <= t[None, None, :, None, None])[:, :, None]
        s = jnp.einsum('bqgrd,bgqkld->bgrqkl', qc, ksel) * scale
        p = masked_softmax(s, m_sel, (-2, -1))
        o_sel = jnp.einsum('bgrqkl,bgqkld->bqgrd', p.astype(vsel.dtype), vsel)
        kwc = lax.dynamic_slice_in_dim(kw_pad, c * QUERY_CHUNK, WINDOW + QUERY_CHUNK, axis=1)
        vwc = lax.dynamic_slice_in_dim(vw_pad, c * QUERY_CHUNK, WINDOW + QUERY_CHUNK, axis=1)
        kpos = c * QUERY_CHUNK - WINDOW + win_off
        m_win = (kpos[None, :] >= 0) & (kpos[None, :] <= t[:, None]) & (t[:, None] - kpos[None, :] < WINDOW)
        s = jnp.einsum('bqgrd,bkgd->bgrqk', qc, kwc) * scale
        p = masked_softmax(s, m_win, -1)
        o_win = jnp.einsum('bgrqk,bkgd->bqgrd', p.astype(vwc.dtype), vwc)
        return gc[..., 0:1] * o_cmp + gc[..., 1:2] * o_sel + gc[..., 2:3] * o_win

    out = lax.map(chunk, (jnp.arange(n_chunks), q_ch, g_ch))
    return out.swapaxes(0, 1).reshape(B, S, ATTN_W)


def causal_dwconv(x, w, b):
    C = x.shape[-1]
    y = lax.conv_general_dilated(x, w[:, None, :].astype(x.dtype), window_strides=(1,),
                                 padding=[(CONV_WIDTH - 1, 0)],
                                 dimension_numbers=('NWC', 'WIO', 'NWC'),
                                 feature_group_count=C)
    return y + b


def setup_inputs(seed: int = 0) -> dict:
    key = jax.random.key(seed)
    ks = jax.random.split(key, 20)
    f32 = jnp.float32

    def nrm(k, shape, s):
        return jax.random.normal(k, shape, f32) * s

    x = jax.random.normal(ks[0], (BATCH, SEQ, D_MODEL), f32)
    offsets = jax.random.randint(ks[1], (BATCH, 1), 0, 4096, dtype=jnp.int32)
    positions = (jnp.arange(SEQ, dtype=jnp.int32)[None, :] + offsets).astype(jnp.int32)
    return {
        "x": x,
        "positions": positions,
        "mix_norm_g": 1.0 + nrm(ks[2], (DEPTH, D_MODEL), 0.05),
        "w_in": nrm(ks[3], (DEPTH, D_MODEL, IN_W), D_MODEL ** -0.5),
        "q_norm_g": 1.0 + nrm(ks[4], (DEPTH, HEAD_DIM), 0.05),
        "k_norm_g": 1.0 + nrm(ks[5], (DEPTH, 3, HEAD_DIM), 0.05),
        "cmp_pe": nrm(ks[6], (DEPTH, 2, CMP_BLOCK, HEAD_DIM), 0.1),
        "cmp_w1": nrm(ks[7], (DEPTH, 2, CMP_BLOCK, HEAD_DIM, CMP_HIDDEN), (CMP_BLOCK * HEAD_DIM) ** -0.5),
        "cmp_w2": nrm(ks[8], (DEPTH, 2, CMP_HIDDEN, HEAD_DIM), CMP_HIDDEN ** -0.5),
        "pool_w": nrm(ks[9], (DEPTH, POOL_GROUPS, POOL_GROUP_W, POOL_GROUP_W), POOL_GROUP_W ** -0.5),
        "pool_scale": 1.0 + nrm(ks[10], (DEPTH, POOL_W), 0.1),
        "w_pool_out": nrm(ks[11], (DEPTH, POOL_W, D_MODEL), POOL_W ** -0.5),
        "w_nsa_out": nrm(ks[12], (DEPTH, ATTN_W, D_MODEL), ATTN_W ** -0.5),
        "w_out": nrm(ks[13], (DEPTH, D_MODEL, D_MODEL), D_MODEL ** -0.5),
        "ffn_norm_g": 1.0 + nrm(ks[14], (DEPTH, D_MODEL), 0.05),
        "w_up": nrm(ks[15], (DEPTH, D_MODEL, 2 * D_FF), D_MODEL ** -0.5),
        "conv_w": nrm(ks[16], (DEPTH, CONV_WIDTH, D_FF), CONV_WIDTH ** -0.5),
        "conv_b": nrm(ks[17], (DEPTH, D_FF), 0.01),
        "w_down": nrm(ks[18], (DEPTH, D_FF, D_MODEL), D_FF ** -0.5),
    }


def reference(x, positions, mix_norm_g, w_in, q_norm_g, k_norm_g, cmp_pe, cmp_w1, cmp_w2,
              pool_w, pool_scale, w_pool_out, w_nsa_out, w_out, ffn_norm_g, w_up, conv_w,
              conv_b, w_down):
    B, S = x.shape[0], x.shape[1]
    split_at = [POOL_W, POOL_W + ATTN_W, POOL_W + ATTN_W + N_KV_SLOTS * KV_W,
                POOL_W + ATTN_W + N_KV_SLOTS * KV_W + N_HEADS * N_NSA_BRANCHES]
    for l in range(DEPTH):
        h = rms_norm(x, mix_norm_g[l])
        proj = h @ w_in[l]
        u, q, kv, nsa_g, merge_g = jnp.split(proj, split_at, axis=-1)
        q = q.reshape(B, S, N_HEADS, HEAD_DIM)
        kv = kv.reshape(B, S, N_KV_SLOTS, N_KV_GROUPS, HEAD_DIM)
        nsa_gates = jax.nn.sigmoid(nsa_g.reshape(B, S, N_HEADS, N_NSA_BRANCHES))
        gate_pool, gate_nsa = jnp.split(jax.nn.sigmoid(merge_g), N_MERGE, axis=-1)
        y_pool = pooling_mixer(u, pool_w[l], pool_scale[l]) @ w_pool_out[l]
        y_nsa = nsa_mixer(q, kv[:, :, 0], kv[:, :, 1], kv[:, :, 2], kv[:, :, 3], kv[:, :, 4], kv[:, :, 5],
                          nsa_gates, positions, q_norm_g[l], k_norm_g[l], cmp_pe[l], cmp_w1[l],
                          cmp_w2[l]) @ w_nsa_out[l]
        x = x + (gate_pool * y_pool + gate_nsa * y_nsa) @ w_out[l]
        h = rms_norm(x, ffn_norm_g[l])
        gate_pre, val = jnp.split(h @ w_up[l], 2, axis=-1)
        gate_c = causal_dwconv(gate_pre, conv_w[l], conv_b[l])
        x = x + (jax.nn.silu(gate_c) * val) @ w_down[l]
    return x
```

```python
import functools

import numpy as np
import jax
import jax.numpy as jnp
from jax import lax
from jax.experimental import pallas as pl
from jax.experimental.pallas import tpu as pltpu

D_MODEL = 1024
POOL_WINDOWS = (2, 4, 8, 16)
POOL_GROUP_W = 128
POOL_W = len(POOL_WINDOWS) * POOL_GROUP_W
POOL_HALO = max(POOL_WINDOWS)
N_HEADS = 8
HEAD_DIM = 64
N_KV_GROUPS = 2
HEADS_PER_GROUP = N_HEADS // N_KV_GROUPS
ATTN_W = N_HEADS * HEAD_DIM
KV_W = N_KV_GROUPS * HEAD_DIM
N_KV_SLOTS = 6
N_NSA_BRANCHES = 3
CMP_BLOCK = 32
CMP_STRIDE = 16
CMP_HIDDEN = 128
SEL_BLOCK = 64
N_SELECT = 16
WINDOW = 512
ROPE_THETA = 500000.0
ROT_DIM = HEAD_DIM // 4
ROT_HALF = ROT_DIM // 2
D_FF = 2816
CONV_WIDTH = 3
RMS_EPS = 1e-6
NEG_INF = -1e30
SEL_FORCE = 1e9

LANES = 128
SUBLANES = 8
VMEM_LIMIT = 56 * 1024 * 1024

TM_PROJ = 512
TQ = 128
TK = 256
TM_FFN = 512
FF_CHUNK = 256

BF16 = jnp.bfloat16
F32 = jnp.float32


def _lane_iota(shape):
    return lax.broadcasted_iota(jnp.int32, shape, len(shape) - 1)


def _row_iota(shape):
    return lax.broadcasted_iota(jnp.int32, shape, len(shape) - 2)


def _dot(a, b):
    return jnp.dot(a, b, preferred_element_type=F32)


def _dot_nt(a, b):
    return lax.dot_general(a, b, (((1,), (1,)), ((), ())), preferred_element_type=F32)


def _half_norm(v, gain_row):
    lo = _lane_iota(v.shape) < HEAD_DIM
    sq = v * v
    ss_lo = jnp.sum(jnp.where(lo, sq, 0.0), axis=-1, keepdims=True)
    ss_hi = jnp.sum(jnp.where(lo, 0.0, sq), axis=-1, keepdims=True)
    ms = jnp.where(lo, ss_lo, ss_hi) * (1.0 / HEAD_DIM)
    return v * lax.rsqrt(ms + RMS_EPS) * gain_row


def _rope_factors(pos_col, inv_row):
    ang = pos_col.astype(F32) * inv_row
    d = _lane_iota(ang.shape) & (HEAD_DIM - 1)
    cos = jnp.where(d < ROT_DIM, jnp.cos(ang), 1.0)
    sin = jnp.sin(ang)
    sin = jnp.where(d < ROT_HALF, -sin, jnp.where(d < ROT_DIM, sin, 0.0))
    return cos, sin


def _rope(v, cos, sin):
    d = _lane_iota(v.shape) & (HEAD_DIM - 1)
    partner = jnp.where(d < ROT_HALF,
                        pltpu.roll(v, LANES - ROT_HALF, 1),
                        pltpu.roll(v, ROT_HALF, 1))
    return v * cos + partner * sin


def _in_proj_kernel(x_ref, pos_ref, g_ref, wu_ref, wq_ref, wkv_ref, wmg_ref, wng_ref,
                    inv_ref, qg_ref, kg_ref, pw_ref, ps_ref, wpo_ref,
                    q_out, kc_out, vc_out, ks_out, vs_out, kw_out, vw_out,
                    gate_out, zp_out, gn_out, ext_ref):
    si = pl.program_id(1)
    tm = x_ref.shape[0]

    x = x_ref[...]
    ms = jnp.mean(x * x, axis=-1, keepdims=True)
    hb = (x * lax.rsqrt(ms + RMS_EPS) * g_ref[...]).astype(BF16)

    u = _dot(hb, wu_ref[...])

    @pl.when(si == 0)
    def _():
        ext_ref[0:POOL_HALO, :] = jnp.zeros((POOL_HALO, POOL_W), F32)

    ext_ref[POOL_HALO:, :] = u
    t = si * tm + _row_iota((tm, 1))
    count = (t + 1).astype(F32)
    pooled = []
    for gi, w in enumerate(POOL_WINDOWS):
        sl = slice(gi * POOL_GROUP_W, (gi + 1) * POOL_GROUP_W)
        acc = ext_ref[:, sl]
        step = 1
        while step < w:
            acc = acc + pltpu.roll(acc, step, 0)
            step *= 2
        win_sum = acc[POOL_HALO:, :]
        mean = win_sum / jnp.minimum(count, float(w))
        pm = (mean - u[:, sl]).astype(BF16)
        pooled.append((_dot(pm, pw_ref[gi]) * ps_ref[:, sl]).astype(BF16))
    ext_ref[0:POOL_HALO, :] = ext_ref[tm:tm + POOL_HALO, :]
    y_pool = _dot(jnp.concatenate(pooled, axis=-1), wpo_ref[...])

    mg = _dot(hb, wmg_ref[...])
    zp_out[...] = (jax.nn.sigmoid(mg[:, :D_MODEL]) * y_pool).astype(BF16)
    gn_out[...] = jax.nn.sigmoid(mg[:, D_MODEL:]).astype(BF16)
    gate_out[...] = jax.nn.sigmoid(_dot(hb, wng_ref[...]))

    cos, sin = _rope_factors(pos_ref[...], inv_ref[...])
    q = _dot(hb, wq_ref[...])
    scale = HEAD_DIM ** -0.5
    for j in range(ATTN_W // LANES):
        sl = slice(j * LANES, (j + 1) * LANES)
        qn = _rope(_half_norm(q[:, sl], qg_ref[...]), cos, sin)
        q_out[:, sl] = (qn * scale).astype(BF16)

    kv = _dot(hb, wkv_ref[...])
    kc_out[...] = kv[:, 0 * KV_W:1 * KV_W].astype(BF16)
    vc_out[...] = kv[:, 1 * KV_W:2 * KV_W].astype(BF16)
    ks = _half_norm(kv[:, 2 * KV_W:3 * KV_W], kg_ref[1:2, :])
    ks_out[...] = _rope(ks, cos, sin).astype(BF16)
    vs_out[...] = kv[:, 3 * KV_W:4 * KV_W].astype(BF16)
    kw = _half_norm(kv[:, 4 * KV_W:5 * KV_W], kg_ref[2:3, :])
    kw_out[...] = _rope(kw, cos, sin).astype(BF16)
    vw_out[...] = kv[:, 5 * KV_W:6 * KV_W].astype(BF16)


def _const_spec(shape):
    nd = len(shape)
    return pl.BlockSpec(shape, lambda b, s: (0,) * nd, pipeline_mode=pl.Buffered(1))


def _in_proj(x, pos3, g_row, wu, wq, wkv, wmg, wng, inv_row, qg_row, kg_rows, pw, ps_row, wpo):
    B, S, _ = x.shape
    tm = TM_PROJ
    tok = lambda w: pl.BlockSpec((None, tm, w), lambda b, s: (b, s, 0))
    out_shapes = (
        jax.ShapeDtypeStruct((B, S, ATTN_W), BF16),
        jax.ShapeDtypeStruct((B, S, KV_W), BF16),
        jax.ShapeDtypeStruct((B, S, KV_W), BF16),
        jax.ShapeDtypeStruct((B, S, KV_W), BF16),
        jax.ShapeDtypeStruct((B, S, KV_W), BF16),
        jax.ShapeDtypeStruct((B, S, KV_W), BF16),
        jax.ShapeDtypeStruct((B, S, KV_W), BF16),
        jax.ShapeDtypeStruct((B, S, LANES), F32),
        jax.ShapeDtypeStruct((B, S, D_MODEL), BF16),
        jax.ShapeDtypeStruct((B, S, D_MODEL), BF16),
    )
    out_specs = (tok(ATTN_W),) + (tok(KV_W),) * 6 + (tok(LANES), tok(D_MODEL), tok(D_MODEL))
    in_specs = [
        tok(D_MODEL),
        pl.BlockSpec((None, tm, 1), lambda b, s: (b, s, 0)),
        _const_spec(g_row.shape), _const_spec(wu.shape), _const_spec(wq.shape),
        _const_spec(wkv.shape), _const_spec(wmg.shape), _const_spec(wng.shape),
        _const_spec(inv_row.shape), _const_spec(qg_row.shape), _const_spec(kg_rows.shape),
        _const_spec(pw.shape), _const_spec(ps_row.shape), _const_spec(wpo.shape),
    ]
    return pl.pallas_call(
        _in_proj_kernel,
        grid=(B, S // tm),
        in_specs=in_specs,
        out_specs=out_specs,
        out_shape=out_shapes,
        scratch_shapes=[pltpu.VMEM((POOL_HALO + tm, POOL_W), F32)],
        compiler_params=pltpu.CompilerParams(
            dimension_semantics=("arbitrary", "arbitrary"),
            vmem_limit_bytes=VMEM_LIMIT),
        name="in_proj",
    )(x, pos3, g_row, wu, wq, wkv, wmg, wng, inv_row, qg_row, kg_rows, pw, ps_row, wpo)


def _compress(raw_ref, pe_ref, w1_ref, w2_ref):
    a = raw_ref[...].astype(F32)
    first = _dot((a + pe_ref[0:1, :]).astype(BF16), w1_ref[0])
    second = _dot((a + pe_ref[1:2, :]).astype(BF16), w1_ref[1])
    nblk = first.shape[0]
    hid = first + pltpu.roll(second, nblk - 1, 0)
    hid = hid * jax.nn.sigmoid(hid)
    return _dot(hid.astype(BF16), w2_ref[...])


def _attend(qs, k_ref, v_ref, kt_lo, kt_hi, bias_fn, m_ref, l_ref, acc_ref):
    m_ref[...] = jnp.full(m_ref.shape, NEG_INF, F32)
    l_ref[...] = jnp.zeros(l_ref.shape, F32)
    acc_ref[...] = jnp.zeros(acc_ref.shape, F32)
    rows = qs.shape[0]

    def body(kt, carry):
        start = pl.multiple_of(kt * TK, TK)
        k = k_ref[pl.ds(start, TK), :]
        v = v_ref[pl.ds(start, TK), :]
        s = _dot_nt(qs, k)
        bias = bias_fn(kt)
        s = (s.reshape(bias.shape[0], -1, TQ, TK) + bias[:, None]).reshape(rows, TK)
        m_old = m_ref[...]
        m_new = jnp.maximum(m_old, jnp.max(s, axis=-1, keepdims=True))
        alpha = jnp.exp(m_old - m_new)
        p = jnp.exp(s - m_new)
        l_ref[...] = alpha * l_ref[...] + jnp.sum(p, axis=-1, keepdims=True)
        acc_ref[...] = alpha * acc_ref[...] + _dot(p.astype(BF16), v)
        m_ref[...] = m_new
        return carry

    lax.fori_loop(kt_lo, kt_hi, body, 0)
    return acc_ref[...] / l_ref[...]


def _nsa_kernel(q_ref, kcr_ref, vcr_ref, ks_ref, vs_ref, kw_ref, vw_ref, gate_ref, posc_ref,
                pek_ref, pev_ref, w1k_ref, w1v_ref, w2k_ref, w2v_ref, kg_ref, inv_ref,
                ov_ref, exp_ref,
                o_ref, kc_s, vc_s, m_ref, l_ref, acc_ref):
    qi = pl.program_id(1)

    @pl.when(qi == 0)
    def _():
        kc = _compress(kcr_ref, pek_ref, w1k_ref, w2k_ref)
        cos, sin = _rope_factors(posc_ref[...], inv_ref[...])
        kc_s[...] = _rope(_half_norm(kc, kg_ref[...]), cos, sin).astype(BF16)
        vc_s[...] = _compress(vcr_ref, pev_ref, w1v_ref, w2v_ref).astype(BF16)

    t0 = qi * TQ
    t = t0 + _row_iota((TQ, 1))
    lane = _lane_iota((TQ, LANES))
    lo = lane < HEAD_DIM

    q = q_ref[...]
    zero = jnp.zeros((TQ, LANES), BF16)
    slabs = [q[:, j * LANES:(j + 1) * LANES] for j in range(HEADS_PER_GROUP)]
    qs = jnp.concatenate([jnp.where(lo, s, zero) for s in slabs]
                         + [jnp.where(lo, zero, s) for s in slabs], axis=0)
    rows = N_HEADS * TQ

    s = _dot_nt(qs, kc_s[...])
    vis = lane * CMP_STRIDE + (CMP_BLOCK - 1) <= t
    s = (s.reshape(N_HEADS, TQ, LANES) + jnp.where(vis, 0.0, NEG_INF)[None]).reshape(rows, LANES)
    p = jnp.exp(s - jnp.max(s, axis=-1, keepdims=True))
    p = p / jnp.sum(p, axis=-1, keepdims=True)
    p = p.reshape(N_HEADS, TQ, LANES)
    p = jnp.where((t >= CMP_BLOCK - 1)[None], p, 0.0)
    o_cmp = _dot(p.reshape(rows, LANES).astype(BF16), vc_s[...])

    n_sel = ks_ref.shape[0] // SEL_BLOCK
    jl = lane & (n_sel - 1)
    cur = t // SEL_BLOCK
    forced = (jl == 0) | (jl == cur) | (jl == cur - 1)
    future = jl > cur
    sel01 = []
    for g in range(N_KV_GROUPS):
        psum = jnp.sum(p[g * HEADS_PER_GROUP:(g + 1) * HEADS_PER_GROUP], axis=0)
        p_hi = psum.astype(BF16)
        p_lo = (psum - p_hi.astype(F32)).astype(BF16)
        imp = _dot(p_hi, ov_ref[...]) + _dot(p_lo, ov_ref[...])
        val = jnp.where(forced, SEL_FORCE, jnp.where(future, NEG_INF, imp))
        rank = jnp.zeros((TQ, LANES), jnp.int32)
        for d in range(1, n_sel):
            other = pltpu.roll(val, d, 1)
            ahead = (other > val) | ((other == val) & (jl >= d))
            rank = rank + ahead.astype(jnp.int32)
        chosen = (rank < N_SELECT) & (lane < n_sel)
        sel01.append(jnp.where(chosen, 1.0, 0.0).astype(BF16))

    key_off = _lane_iota((TQ, TK))

    def sel_bias(kt):
        e = exp_ref[kt]
        causal = kt * TK + key_off <= t
        return jnp.stack([jnp.where((_dot(sel01[g], e) > 0.5) & causal, 0.0, NEG_INF)
                          for g in range(N_KV_GROUPS)], axis=0)

    def win_bias(kt):
        dist = t - (kt * TK + key_off)
        return jnp.where((dist >= 0) & (dist < WINDOW), 0.0, NEG_INF)[None]

    kt_hi = (t0 + TQ - 1) // TK + 1
    o_sel = _attend(qs, ks_ref, vs_ref, 0, kt_hi, sel_bias, m_ref, l_ref, acc_ref)
    kt_lo = jnp.maximum(t0 - (WINDOW - 1), 0) // TK
    o_win = _attend(qs, kw_ref, vw_ref, kt_lo, kt_hi, win_bias, m_ref, l_ref, acc_ref)

    gates = gate_ref[...]
    heads = []
    for h in range(N_HEADS):
        r = slice(h * TQ, (h + 1) * TQ)
        c = N_NSA_BRANCHES * h
        heads.append(gates[:, c:c + 1] * o_cmp[r] + gates[:, c + 1:c + 2] * o_sel[r]
                     + gates[:, c + 2:c + 3] * o_win[r])
    for j in range(HEADS_PER_GROUP):
        o_ref[:, j * LANES:(j + 1) * LANES] = jnp.where(
            lo, heads[j], heads[HEADS_PER_GROUP + j]).astype(BF16)


def _nsa(q, kc_raw, vc_raw, ks, vs, kw, vw, gates, pos_cmp, pek, pev, w1k, w1v, w2k, w2v,
         kg_row, inv_row, ov, expand):
    B, S, _ = q.shape
    n_rows = S // CMP_STRIDE
    kv_spec = pl.BlockSpec((None, S, KV_W), lambda b, i: (b, 0, 0))
    raw_spec = pl.BlockSpec((None, n_rows, CMP_STRIDE * KV_W), lambda b, i: (b, 0, 0))
    consts = (pek, pev, w1k, w1v, w2k, w2v, kg_row, inv_row, ov, expand)
    in_specs = [
        pl.BlockSpec((None, TQ, ATTN_W), lambda b, i: (b, i, 0)),
        raw_spec, raw_spec, kv_spec, kv_spec, kv_spec, kv_spec,
        pl.BlockSpec((None, TQ, LANES), lambda b, i: (b, i, 0)),
        pl.BlockSpec((None, n_rows, 1), lambda b, i: (b, 0, 0)),
    ] + [_const_spec(c.shape) for c in consts]
    rows = N_HEADS * TQ
    return pl.pallas_call(
        _nsa_kernel,
        grid=(B, S // TQ),
        in_specs=in_specs,
        out_specs=pl.BlockSpec((None, TQ, ATTN_W), lambda b, i: (b, i, 0)),
        out_shape=jax.ShapeDtypeStruct((B, S, ATTN_W), BF16),
        scratch_shapes=[
            pltpu.VMEM((n_rows, KV_W), BF16),
            pltpu.VMEM((n_rows, KV_W), BF16),
            pltpu.VMEM((rows, 1), F32),
            pltpu.VMEM((rows, 1), F32),
            pltpu.VMEM((rows, LANES), F32),
        ],
        compiler_params=pltpu.CompilerParams(
            dimension_semantics=("arbitrary", "arbitrary"),
            vmem_limit_bytes=VMEM_LIMIT),
        name="nsa",
    )(q, kc_raw.reshape(B, n_rows, CMP_STRIDE * KV_W), vc_raw.reshape(B, n_rows, CMP_STRIDE * KV_W),
      ks, vs, kw, vw, gates, pos_cmp, *consts)


def _ffn_kernel(x_ref, zp_ref, gn_ref, a_ref, wno_ref, wo_ref, g_ref, wug_ref, wuv_ref,
                cw_ref, cb_ref, wd_ref, o_ref, carry_ref):
    si = pl.program_id(1)
    tm = x_ref.shape[0]

    y_nsa = _dot(a_ref[...], wno_ref[...])
    merged = zp_ref[...].astype(F32) + gn_ref[...].astype(F32) * y_nsa
    x1 = x_ref[...] + _dot(merged.astype(BF16), wo_ref[...])
    ms = jnp.mean(x1 * x1, axis=-1, keepdims=True)
    hb = (x1 * lax.rsqrt(ms + RMS_EPS) * g_ref[...]).astype(BF16)

    @pl.when(si == 0)
    def _():
        carry_ref[...] = jnp.zeros(carry_ref.shape, F32)

    row = _row_iota((tm, FF_CHUNK))
    acc = x1
    for c in range(D_FF // FF_CHUNK):
        sl = slice(c * FF_CHUNK, (c + 1) * FF_CHUNK)
        gp = _dot(hb, wug_ref[:, sl])
        val = _dot(hb, wuv_ref[:, sl])
        prev = carry_ref[:, sl]
        p1 = prev[SUBLANES - 1:SUBLANES, :]
        p2 = prev[SUBLANES - 2:SUBLANES - 1, :]
        lag1 = jnp.where(row == 0, p1, pltpu.roll(gp, 1, 0))
        lag2 = jnp.where(row == 0, p2, jnp.where(row == 1, p1, pltpu.roll(gp, 2, 0)))
        carry_ref[:, sl] = gp[tm - SUBLANES:, :]
        gc = (cw_ref[0:1, sl] * lag2 + cw_ref[1:2, sl] * lag1 + cw_ref[2:3, sl] * gp
              + cb_ref[:, sl])
        act = (gc * jax.nn.sigmoid(gc) * val).astype(BF16)
        acc = acc + _dot(act, wd_ref[sl, :])
    o_ref[...] = acc


def _ffn(x, zp, gn, attn, wno, wo, g_row, wug, wuv, cw, cb_row, wd):
    B, S, _ = x.shape
    tm = TM_FFN
    tok = lambda w: pl.BlockSpec((None, tm, w), lambda b, s: (b, s, 0))
    consts = (wno, wo, g_row, wug, wuv, cw, cb_row, wd)
    return pl.pallas_call(
        _ffn_kernel,
        grid=(B, S // tm),
        in_specs=[tok(D_MODEL), tok(D_MODEL), tok(D_MODEL), tok(ATTN_W)]
                 + [_const_spec(c.shape) for c in consts],
        out_specs=tok(D_MODEL),
        out_shape=jax.ShapeDtypeStruct((B, S, D_MODEL), F32),
        scratch_shapes=[pltpu.VMEM((SUBLANES, D_FF), F32)],
        compiler_params=pltpu.CompilerParams(
            dimension_semantics=("arbitrary", "arbitrary"),
            vmem_limit_bytes=VMEM_LIMIT),
        name="merge_ffn",
    )(x, zp, gn, attn, *consts)


def _slab_perm():
    j, half, d = np.meshgrid(np.arange(HEADS_PER_GROUP), np.arange(N_KV_GROUPS),
                             np.arange(HEAD_DIM), indexing="ij")
    return ((j + HEADS_PER_GROUP * half) * HEAD_DIM + d).reshape(-1)


def _group_block_diag(w):
    z = jnp.zeros_like(w)
    return jnp.concatenate([jnp.concatenate([w, z], axis=-1),
                            jnp.concatenate([z, w], axis=-1)], axis=-2)


def _compress_weights(pe, w1, w2):
    half = CMP_BLOCK // 2
    pe_rows = jnp.tile(pe.reshape(2, half, 1, HEAD_DIM), (1, 1, N_KV_GROUPS, 1)).reshape(2, -1)
    w1h = w1.reshape(2, half, HEAD_DIM, CMP_HIDDEN)
    w1bd = _group_block_diag(w1h).reshape(2, half * KV_W, N_KV_GROUPS * CMP_HIDDEN)
    return pe_rows.astype(F32), w1bd.astype(BF16), _group_block_diag(w2).astype(BF16)


def kernel(x, positions, mix_norm_g, w_in, q_norm_g, k_norm_g, cmp_pe, cmp_w1, cmp_w2, pool_w,
           pool_scale, w_pool_out, w_nsa_out, w_out, ffn_norm_g, w_up, conv_w, conv_b, w_down):
    B, S, _ = x.shape
    depth = w_in.shape[0]
    perm = _slab_perm()
    o_q = POOL_W
    o_kv = o_q + ATTN_W
    o_ng = o_kv + N_KV_SLOTS * KV_W
    o_mg = o_ng + N_HEADS * N_NSA_BRANCHES
    half = jnp.arange(ROT_HALF, dtype=F32)
    inv = ROPE_THETA ** (-(half * 2.0 / ROT_DIM))
    inv_row = jnp.tile(inv, LANES // ROT_HALF)[None, :]
    pos3 = positions[:, :, None]
    n_cmp_rows = S // CMP_STRIDE
    pos_cmp = positions[:, CMP_BLOCK - 1::CMP_STRIDE]
    pos_cmp = jnp.pad(pos_cmp, ((0, 0), (0, n_cmp_rows - pos_cmp.shape[1])))[:, :, None]
    n_sel = S // SEL_BLOCK
    n_cmp = (S - CMP_BLOCK) // CMP_STRIDE + 1
    sel_start = np.arange(n_sel)[:, None] * SEL_BLOCK
    cmp_start = np.arange(n_cmp)[None, :] * CMP_STRIDE
    ov = np.clip(np.minimum(sel_start + SEL_BLOCK, cmp_start + CMP_BLOCK)
                 - np.maximum(sel_start, cmp_start), 0, None) / CMP_BLOCK
    ov4 = np.zeros((n_cmp_rows, LANES), np.float32)
    ov4[:n_cmp] = np.tile(ov.T, (1, LANES // n_sel))
    expand = np.zeros((S // TK, LANES, TK), np.float32)
    keys = np.arange(S)
    expand[keys // TK, keys // SEL_BLOCK, keys % TK] = 1.0
    ov4 = jnp.asarray(ov4, BF16)
    expand = jnp.asarray(expand, BF16)

    for l in range(depth):
        w = w_in[l]
        wng = jnp.pad(w[:, o_ng:o_mg], ((0, 0), (0, LANES - (o_mg - o_ng))))
        q, kc_raw, vc_raw, ks, vs, kw, vw, gates, zp, gn = _in_proj(
            x, pos3, mix_norm_g[l][None, :],
            w[:, :o_q].astype(BF16), w[:, o_q:o_kv][:, perm].astype(BF16),
            w[:, o_kv:o_ng].astype(BF16), w[:, o_mg:].astype(BF16), wng.astype(BF16),
            inv_row, jnp.tile(q_norm_g[l], N_KV_GROUPS)[None, :],
            jnp.tile(k_norm_g[l], (1, N_KV_GROUPS)),
            pool_w[l].astype(BF16), pool_scale[l][None, :], w_pool_out[l].astype(BF16))
        pek, w1k, w2k = _compress_weights(cmp_pe[l, 0], cmp_w1[l, 0], cmp_w2[l, 0])
        pev, w1v, w2v = _compress_weights(cmp_pe[l, 1], cmp_w1[l, 1], cmp_w2[l, 1])
        attn = _nsa(q, kc_raw, vc_raw, ks, vs, kw, vw, gates, pos_cmp, pek, pev, w1k, w1v,
                    w2k, w2v, jnp.tile(k_norm_g[l, 0], N_KV_GROUPS)[None, :], inv_row, ov4, expand)
        x = _ffn(x, zp, gn, attn, w_nsa_out[l][perm, :].astype(BF16), w_out[l].astype(BF16),
                 ffn_norm_g[l][None, :], w_up[l][:, :D_FF].astype(BF16),
                 w_up[l][:, D_FF:].astype(BF16), conv_w[l], conv_b[l][None, :],
                 w_down[l].astype(BF16))
    return x
```

```python
import functools

import numpy as np
import jax
import jax.numpy as jnp
from jax import lax
from jax.experimental import pallas as pl
from jax.experimental.pallas import tpu as pltpu

D_MODEL = 1024
POOL_WINDOWS = (2, 4, 8, 16)
POOL_GROUP_W = 128
POOL_W = len(POOL_WINDOWS) * POOL_GROUP_W
POOL_HALO = max(POOL_WINDOWS)
N_HEADS = 8
HEAD_DIM = 64
N_KV_GROUPS = 2
HEADS_PER_GROUP = N_HEADS // N_KV_GROUPS
ATTN_W = N_HEADS * HEAD_DIM
KV_W = N_KV_GROUPS * HEAD_DIM
N_KV_SLOTS = 6
N_NSA_BRANCHES = 3
CMP_BLOCK = 32
CMP_STRIDE = 16
CMP_HIDDEN = 128
SEL_BLOCK = 64
N_SELECT = 16
WINDOW = 512
ROPE_THETA = 500000.0
ROT_DIM = HEAD_DIM // 4
ROT_HALF = ROT_DIM // 2
D_FF = 2816
CONV_WIDTH = 3
RMS_EPS = 1e-6
NEG_INF = -1e30
SEL_FORCE = 1e9

LANES = 128
SUBLANES = 8
VMEM_LIMIT = 56 * 1024 * 1024

TM_PROJ = 512
TQ = 128
TK = 512
TM_FFN = 512
FF_CHUNK = 256

BF16 = jnp.bfloat16
F32 = jnp.float32


def _lane_iota(shape):
    return lax.broadcasted_iota(jnp.int32, shape, len(shape) - 1)


def _row_iota(shape):
    return lax.broadcasted_iota(jnp.int32, shape, len(shape) - 2)


def _dot(a, b):
    return jnp.dot(a, b, preferred_element_type=F32)


def _dot_nt(a, b):
    return lax.dot_general(a, b, (((1,), (1,)), ((), ())), preferred_element_type=F32)


def _half_norm(v, gain_row):
    lo = _lane_iota(v.shape) < HEAD_DIM
    sq = v * v
    ss_lo = jnp.sum(jnp.where(lo, sq, 0.0), axis=-1, keepdims=True)
    ss_hi = jnp.sum(jnp.where(lo, 0.0, sq), axis=-1, keepdims=True)
    ms = jnp.where(lo, ss_lo, ss_hi) * (1.0 / HEAD_DIM)
    return v * lax.rsqrt(ms + RMS_EPS) * gain_row


def _rope_factors(pos_col, inv_row):
    ang = pos_col.astype(F32) * inv_row
    d = _lane_iota(ang.shape) & (HEAD_DIM - 1)
    cos = jnp.where(d < ROT_DIM, jnp.cos(ang), 1.0)
    sin = jnp.sin(ang)
    sin = jnp.where(d < ROT_HALF, -sin, jnp.where(d < ROT_DIM, sin, 0.0))
    return cos, sin


def _rope(v, cos, sin):
    d = _lane_iota(v.shape) & (HEAD_DIM - 1)
    partner = jnp.where(d < ROT_HALF,
                        pltpu.roll(v, LANES - ROT_HALF, 1),
                        pltpu.roll(v, ROT_HALF, 1))
    return v * cos + partner * sin


def _in_proj_kernel(x_ref, pos_ref, g_ref, wu_ref, wq_ref, wkv_ref, wmg_ref, wng_ref,
                    inv_ref, qg_ref, kg_ref, pw_ref, ps_ref, wpo_ref,
                    q_out, kc_out, vc_out, ks_out, vs0_out, vs1_out, kw_out, vw0_out, vw1_out,
                    gate_out, zp_out, gn_out, ext_ref):
    si = pl.program_id(1)
    tm = x_ref.shape[0]

    x = x_ref[...]
    ms = jnp.mean(x * x, axis=-1, keepdims=True)
    hb = (x * lax.rsqrt(ms + RMS_EPS) * g_ref[...]).astype(BF16)

    u = _dot(hb, wu_ref[...])

    @pl.when(si == 0)
    def _():
        ext_ref[0:POOL_HALO, :] = jnp.zeros((POOL_HALO, POOL_W), F32)

    ext_ref[POOL_HALO:, :] = u
    t = si * tm + _row_iota((tm, 1))
    count = (t + 1).astype(F32)
    pooled = []
    for gi, w in enumerate(POOL_WINDOWS):
        sl = slice(gi * POOL_GROUP_W, (gi + 1) * POOL_GROUP_W)
        acc = ext_ref[:, sl]
        step = 1
        while step < w:
            acc = acc + pltpu.roll(acc, step, 0)
            step *= 2
        win_sum = acc[POOL_HALO:, :]
        mean = win_sum / jnp.minimum(count, float(w))
        pm = (mean - u[:, sl]).astype(BF16)
        pooled.append((_dot(pm, pw_ref[gi]) * ps_ref[:, sl]).astype(BF16))
    ext_ref[0:POOL_HALO, :] = ext_ref[tm:tm + POOL_HALO, :]
    y_pool = _dot(jnp.concatenate(pooled, axis=-1), wpo_ref[...])

    mg = _dot(hb, wmg_ref[...])
    zp_out[...] = (jax.nn.sigmoid(mg[:, :D_MODEL]) * y_pool).astype(BF16)
    gn_out[...] = jax.nn.sigmoid(mg[:, D_MODEL:]).astype(BF16)
    gate_out[...] = jax.nn.sigmoid(_dot(hb, wng_ref[...]))

    cos, sin = _rope_factors(pos_ref[...], inv_ref[...])
    q = _dot(hb, wq_ref[...])
    scale = HEAD_DIM ** -0.5
    for j in range(ATTN_W // LANES):
        sl = slice(j * LANES, (j + 1) * LANES)
        qn = _rope(_half_norm(q[:, sl], qg_ref[...]), cos, sin)
        q_out[:, sl] = (qn * scale).astype(BF16)

    kv = _dot(hb, wkv_ref[...])
    kc_out[...] = kv[:, 0 * KV_W:1 * KV_W].astype(BF16)
    vc_out[...] = kv[:, 1 * KV_W:2 * KV_W].astype(BF16)
    ks = _half_norm(kv[:, 2 * KV_W:3 * KV_W], kg_ref[1:2, :])
    ks_out[...] = _rope(ks, cos, sin).astype(BF16)
    _store_value_ext(kv[:, 3 * KV_W:4 * KV_W], vs0_out, vs1_out)
    kw = _half_norm(kv[:, 4 * KV_W:5 * KV_W], kg_ref[2:3, :])
    kw_out[...] = _rope(kw, cos, sin).astype(BF16)
    _store_value_ext(kv[:, 5 * KV_W:6 * KV_W], vw0_out, vw1_out)


def _store_value_ext(v, g0_out, g1_out):
    lo = _lane_iota(v.shape) < HEAD_DIM
    g0_out[...] = jnp.where(lo, v, 1.0).astype(BF16)
    g1_out[...] = jnp.where(lo, 1.0, v).astype(BF16)


def _const_spec(shape):
    nd = len(shape)
    return pl.BlockSpec(shape, lambda b, s: (0,) * nd, pipeline_mode=pl.Buffered(1))


def _in_proj(x, pos3, g_row, wu, wq, wkv, wmg, wng, inv_row, qg_row, kg_rows, pw, ps_row, wpo):
    B, S, _ = x.shape
    tm = TM_PROJ
    tok = lambda w: pl.BlockSpec((None, tm, w), lambda b, s: (b, s, 0))
    out_shapes = (
        jax.ShapeDtypeStruct((B, S, ATTN_W), BF16),
        jax.ShapeDtypeStruct((B, S, KV_W), BF16),
        jax.ShapeDtypeStruct((B, S, KV_W), BF16),
        jax.ShapeDtypeStruct((B, S, KV_W), BF16),
        jax.ShapeDtypeStruct((B, S, KV_W), BF16),
        jax.ShapeDtypeStruct((B, S, KV_W), BF16),
        jax.ShapeDtypeStruct((B, S, KV_W), BF16),
        jax.ShapeDtypeStruct((B, S, KV_W), BF16),
        jax.ShapeDtypeStruct((B, S, KV_W), BF16),
        jax.ShapeDtypeStruct((B, S, LANES), F32),
        jax.ShapeDtypeStruct((B, S, D_MODEL), BF16),
        jax.ShapeDtypeStruct((B, S, D_MODEL), BF16),
    )
    out_specs = (tok(ATTN_W),) + (tok(KV_W),) * 8 + (tok(LANES), tok(D_MODEL), tok(D_MODEL))
    in_specs = [
        tok(D_MODEL),
        pl.BlockSpec((None, tm, 1), lambda b, s: (b, s, 0)),
        _const_spec(g_row.shape), _const_spec(wu.shape), _const_spec(wq.shape),
        _const_spec(wkv.shape), _const_spec(wmg.shape), _const_spec(wng.shape),
        _const_spec(inv_row.shape), _const_spec(qg_row.shape), _const_spec(kg_rows.shape),
        _const_spec(pw.shape), _const_spec(ps_row.shape), _const_spec(wpo.shape),
    ]
    return pl.pallas_call(
        _in_proj_kernel,
        grid=(B, S // tm),
        in_specs=in_specs,
        out_specs=out_specs,
        out_shape=out_shapes,
        scratch_shapes=[pltpu.VMEM((POOL_HALO + tm, POOL_W), F32)],
        compiler_params=pltpu.CompilerParams(
            dimension_semantics=("arbitrary", "arbitrary"),
            vmem_limit_bytes=VMEM_LIMIT),
        name="in_proj",
    )(x, pos3, g_row, wu, wq, wkv, wmg, wng, inv_row, qg_row, kg_rows, pw, ps_row, wpo)


def _compress(raw_ref, pe_ref, w1_ref, w2_ref):
    a = raw_ref[...].astype(F32)
    first = _dot((a + pe_ref[0:1, :]).astype(BF16), w1_ref[0])
    second = _dot((a + pe_ref[1:2, :]).astype(BF16), w1_ref[1])
    nblk = first.shape[0]
    hid = first + pltpu.roll(second, nblk - 1, 0)
    hid = hid * jax.nn.sigmoid(hid)
    return _dot(hid.astype(BF16), w2_ref[...])


def _normalize(acc):
    return acc / pltpu.roll(acc, HEAD_DIM, 1)


def _nsa_kernel(q_ref, kcr_ref, vcr_ref, ks_ref, vs0_ref, vs1_ref, kw_ref, vw0_ref, vw1_ref,
                gate_ref, posc_ref,
                pek_ref, pev_ref, w1k_ref, w1v_ref, w2k_ref, w2v_ref, kg_ref, inv_ref,
                ovt_ref, exp_ref,
                o_ref, kc_s, vc_s, m_ref, acc_ref):
    qi = pl.program_id(1)
    seq = ks_ref.shape[0]
    grp_rows = HEADS_PER_GROUP * TQ

    @pl.when(qi == 0)
    def _():
        kc = _compress(kcr_ref, pek_ref, w1k_ref, w2k_ref)
        cos, sin = _rope_factors(posc_ref[...], inv_ref[...])
        kc_s[...] = _rope(_half_norm(kc, kg_ref[...]), cos, sin).astype(BF16)
        vc_s[...] = _compress(vcr_ref, pev_ref, w1v_ref, w2v_ref).astype(BF16)

    t0 = qi * TQ
    t = t0 + _row_iota((TQ, 1))
    lane = _lane_iota((TQ, LANES))
    lo = lane < HEAD_DIM

    q = q_ref[...]
    zero = jnp.zeros((TQ, LANES), BF16)
    slabs = [q[:, j * LANES:(j + 1) * LANES] for j in range(HEADS_PER_GROUP)]
    qs = jnp.concatenate([jnp.where(lo, s, zero) for s in slabs]
                         + [jnp.where(lo, zero, s) for s in slabs], axis=0)
    rows = N_HEADS * TQ

    s = _dot_nt(qs, kc_s[...])
    vis = lane * CMP_STRIDE + (CMP_BLOCK - 1) <= t
    s = (s.reshape(N_HEADS, TQ, LANES) + jnp.where(vis, 0.0, NEG_INF)[None]).reshape(rows, LANES)
    p = jnp.exp(s - jnp.max(s, axis=-1, keepdims=True))
    ones = jnp.ones((LANES, LANES), BF16)
    p_hi = p.astype(BF16)
    p_lo = (p - p_hi.astype(F32)).astype(BF16)
    p = p / (_dot(p_hi, ones) + _dot(p_lo, ones))
    p = p.reshape(N_HEADS, TQ, LANES)
    p = jnp.where((t >= CMP_BLOCK - 1)[None], p, 0.0)
    o_cmp = _dot(p.reshape(rows, LANES).astype(BF16), vc_s[...])

    n_sel = seq // SEL_BLOCK
    blk = _row_iota((n_sel, TQ))
    tq_lane = t0 + _lane_iota((n_sel, TQ))
    cur = tq_lane // SEL_BLOCK
    forced = (blk == 0) | (blk == cur) | (blk == cur - 1)
    future = blk > cur
    chosen = []
    for g in range(N_KV_GROUPS):
        psum = jnp.sum(p[g * HEADS_PER_GROUP:(g + 1) * HEADS_PER_GROUP], axis=0)
        ps_hi = psum.astype(BF16)
        ps_lo = (psum - ps_hi.astype(F32)).astype(BF16)
        imp = _dot_nt(ovt_ref[...], ps_hi) + _dot_nt(ovt_ref[...], ps_lo)
        val = jnp.where(forced, SEL_FORCE, jnp.where(future, NEG_INF, imp))
        rank = jnp.zeros((n_sel, TQ), jnp.int32)
        for d in range(1, n_sel):
            other = pltpu.roll(val, d, 0)
            ahead = (other > val) | ((other == val) & (blk >= d))
            rank = rank + ahead.astype(jnp.int32)
        chosen.append(jnp.where(rank < N_SELECT, 1.0, 0.0))
    chosen.append(jnp.zeros((LANES - N_KV_GROUPS * n_sel, TQ), F32))
    eye = jnp.where(_row_iota((TQ, TQ)) == _lane_iota((TQ, TQ)), 1.0, 0.0).astype(BF16)
    sel01 = _dot_nt(eye, jnp.concatenate(chosen, axis=0).astype(BF16)).astype(BF16)

    m_ref[...] = jnp.full(m_ref.shape, NEG_INF, F32)
    acc_ref[...] = jnp.zeros(acc_ref.shape, F32)
    key_off = _lane_iota((TQ, TK))

    def sel_body(kt, carry):
        start = pl.multiple_of(kt * TK, TK)
        k = ks_ref[pl.ds(start, TK), :]
        causal = start + key_off <= t
        for g, v_ref in enumerate((vs0_ref, vs1_ref)):
            r = slice(g * grp_rows, (g + 1) * grp_rows)
            picked = _dot(sel01, exp_ref[g, kt]) > 0.5
            bias = jnp.where(picked & causal, 0.0, NEG_INF)
            s = _dot_nt(qs[r], k)
            s = (s.reshape(HEADS_PER_GROUP, TQ, TK) + bias[None]).reshape(grp_rows, TK)
            m_old = m_ref[r, :]
            m_new = jnp.maximum(m_old, jnp.max(s, axis=-1, keepdims=True))
            alpha = jnp.exp(m_old - m_new)
            p = jnp.exp(s - jnp.tile(m_new, (1, TK // LANES)))
            acc_ref[r, :] = alpha * acc_ref[r, :] + _dot(p.astype(BF16), v_ref[pl.ds(start, TK), :])
            m_ref[r, :] = m_new
        return carry

    lax.fori_loop(0, (t0 + TQ - 1) // TK + 1, sel_body, 0)
    o_sel = _normalize(acc_ref[...])

    n_win = WINDOW + TQ
    start = pl.multiple_of(jnp.maximum(t0 - WINDOW, 0), TQ)
    dist = t - (start + _lane_iota((TQ, n_win)))
    bias = jnp.where((dist >= 0) & (dist < WINDOW), 0.0, NEG_INF)
    s = _dot_nt(qs, kw_ref[pl.ds(start, n_win), :])
    s = (s.reshape(N_HEADS, TQ, n_win) + bias[None]).reshape(rows, n_win)
    p = jnp.exp(s - jnp.max(s, axis=-1, keepdims=True)).astype(BF16)
    o_win = _normalize(jnp.concatenate(
        [_dot(p[g * grp_rows:(g + 1) * grp_rows], v_ref[pl.ds(start, n_win), :])
         for g, v_ref in enumerate((vw0_ref, vw1_ref))], axis=0))

    gates = gate_ref[...]
    heads = []
    for h in range(N_HEADS):
        r = slice(h * TQ, (h + 1) * TQ)
        c = N_NSA_BRANCHES * h
        heads.append(gates[:, c:c + 1] * o_cmp[r] + gates[:, c + 1:c + 2] * o_sel[r]
                     + gates[:, c + 2:c + 3] * o_win[r])
    for j in range(HEADS_PER_GROUP):
        o_ref[:, j * LANES:(j + 1) * LANES] = jnp.where(
            lo, heads[j], heads[HEADS_PER_GROUP + j]).astype(BF16)


def _nsa(q, kc_raw, vc_raw, ks, vs0, vs1, kw, vw0, vw1, gates, pos_cmp, pek, pev, w1k, w1v,
         w2k, w2v, kg_row, inv_row, ovt, expand):
    B, S, _ = q.shape
    assert S % TK == 0 and S >= WINDOW + TQ and LANES >= N_KV_GROUPS * (S // SEL_BLOCK)
    n_rows = S // CMP_STRIDE
    kv_spec = pl.BlockSpec((None, S, KV_W), lambda b, i: (b, 0, 0))
    raw_spec = pl.BlockSpec((None, n_rows, CMP_STRIDE * KV_W), lambda b, i: (b, 0, 0))
    consts = (pek, pev, w1k, w1v, w2k, w2v, kg_row, inv_row, ovt, expand)
    in_specs = [
        pl.BlockSpec((None, TQ, ATTN_W), lambda b, i: (b, i, 0)),
        raw_spec, raw_spec, kv_spec, kv_spec, kv_spec, kv_spec, kv_spec, kv_spec,
        pl.BlockSpec((None, TQ, LANES), lambda b, i: (b, i, 0)),
        pl.BlockSpec((None, n_rows, 1), lambda b, i: (b, 0, 0)),
    ] + [_const_spec(c.shape) for c in consts]
    rows = N_HEADS * TQ
    return pl.pallas_call(
        _nsa_kernel,
        grid=(B, S // TQ),
        in_specs=in_specs,
        out_specs=pl.BlockSpec((None, TQ, ATTN_W), lambda b, i: (b, i, 0)),
        out_shape=jax.ShapeDtypeStruct((B, S, ATTN_W), BF16),
        scratch_shapes=[
            pltpu.VMEM((n_rows, KV_W), BF16),
            pltpu.VMEM((n_rows, KV_W), BF16),
            pltpu.VMEM((rows, LANES), F32),
            pltpu.VMEM((rows, LANES), F32),
        ],
        compiler_params=pltpu.CompilerParams(
            dimension_semantics=("arbitrary", "arbitrary"),
            vmem_limit_bytes=VMEM_LIMIT),
        name="nsa",
    )(q, kc_raw.reshape(B, n_rows, CMP_STRIDE * KV_W), vc_raw.reshape(B, n_rows, CMP_STRIDE * KV_W),
      ks, vs0, vs1, kw, vw0, vw1, gates, pos_cmp, *consts)


def _ffn_kernel(x_ref, zp_ref, gn_ref, a_ref, wno_ref, wo_ref, g_ref, wug_ref, wuv_ref,
                cw_ref, cb_ref, wd_ref, o_ref, carry_ref):
    si = pl.program_id(1)
    tm = x_ref.shape[0]

    y_nsa = _dot(a_ref[...], wno_ref[...])
    merged = zp_ref[...].astype(F32) + gn_ref[...].astype(F32) * y_nsa
    x1 = x_ref[...] + _dot(merged.astype(BF16), wo_ref[...])
    ms = jnp.mean(x1 * x1, axis=-1, keepdims=True)
    hb = (x1 * lax.rsqrt(ms + RMS_EPS) * g_ref[...]).astype(BF16)

    @pl.when(si == 0)
    def _():
        carry_ref[...] = jnp.zeros(carry_ref.shape, F32)

    row = _row_iota((tm, FF_CHUNK))
    acc = x1
    for c in range(D_FF // FF_CHUNK):
        sl = slice(c * FF_CHUNK, (c + 1) * FF_CHUNK)
        gp = _dot(hb, wug_ref[:, sl])
        val = _dot(hb, wuv_ref[:, sl])
        prev = carry_ref[:, sl]
        p1 = prev[SUBLANES - 1:SUBLANES, :]
        p2 = prev[SUBLANES - 2:SUBLANES - 1, :]
        lag1 = jnp.where(row == 0, p1, pltpu.roll(gp, 1, 0))
        lag2 = jnp.where(row == 0, p2, jnp.where(row == 1, p1, pltpu.roll(gp, 2, 0)))
        carry_ref[:, sl] = gp[tm - SUBLANES:, :]
        gc = (cw_ref[0:1, sl] * lag2 + cw_ref[1:2, sl] * lag1 + cw_ref[2:3, sl] * gp
              + cb_ref[:, sl])
        act = (gc * jax.nn.sigmoid(gc) * val).astype(BF16)
        acc = acc + _dot(act, wd_ref[sl, :])
    o_ref[...] = acc


def _ffn(x, zp, gn, attn, wno, wo, g_row, wug, wuv, cw, cb_row, wd):
    B, S, _ = x.shape
    tm = TM_FFN
    tok = lambda w: pl.BlockSpec((None, tm, w), lambda b, s: (b, s, 0))
    consts = (wno, wo, g_row, wug, wuv, cw, cb_row, wd)
    return pl.pallas_call(
        _ffn_kernel,
        grid=(B, S // tm),
        in_specs=[tok(D_MODEL), tok(D_MODEL), tok(D_MODEL), tok(ATTN_W)]
                 + [_const_spec(c.shape) for c in consts],
        out_specs=tok(D_MODEL),
        out_shape=jax.ShapeDtypeStruct((B, S, D_MODEL), F32),
        scratch_shapes=[pltpu.VMEM((SUBLANES, D_FF), F32)],
        compiler_params=pltpu.CompilerParams(
            dimension_semantics=("arbitrary", "arbitrary"),
            vmem_limit_bytes=VMEM_LIMIT),
        name="merge_ffn",
    )(x, zp, gn, attn, *consts)


def _slab_perm():
    j, half, d = np.meshgrid(np.arange(HEADS_PER_GROUP), np.arange(N_KV_GROUPS),
                             np.arange(HEAD_DIM), indexing="ij")
    return ((j + HEADS_PER_GROUP * half) * HEAD_DIM + d).reshape(-1)


def _group_block_diag(w):
    z = jnp.zeros_like(w)
    return jnp.concatenate([jnp.concatenate([w, z], axis=-1),
                            jnp.concatenate([z, w], axis=-1)], axis=-2)


def _compress_weights(pe, w1, w2):
    half = CMP_BLOCK // 2
    pe_rows = jnp.tile(pe.reshape(2, half, 1, HEAD_DIM), (1, 1, N_KV_GROUPS, 1)).reshape(2, -1)
    w1h = w1.reshape(2, half, HEAD_DIM, CMP_HIDDEN)
    w1bd = _group_block_diag(w1h).reshape(2, half * KV_W, N_KV_GROUPS * CMP_HIDDEN)
    return pe_rows.astype(F32), w1bd.astype(BF16), _group_block_diag(w2).astype(BF16)


def kernel(x, positions, mix_norm_g, w_in, q_norm_g, k_norm_g, cmp_pe, cmp_w1, cmp_w2, pool_w,
           pool_scale, w_pool_out, w_nsa_out, w_out, ffn_norm_g, w_up, conv_w, conv_b, w_down):
    B, S, _ = x.shape
    depth = w_in.shape[0]
    perm = _slab_perm()
    o_q = POOL_W
    o_kv = o_q + ATTN_W
    o_ng = o_kv + N_KV_SLOTS * KV_W
    o_mg = o_ng + N_HEADS * N_NSA_BRANCHES
    half = jnp.arange(ROT_HALF, dtype=F32)
    inv = ROPE_THETA ** (-(half * 2.0 / ROT_DIM))
    inv_row = jnp.tile(inv, LANES // ROT_HALF)[None, :]
    pos3 = positions[:, :, None]
    n_cmp_rows = S // CMP_STRIDE
    pos_cmp = positions[:, CMP_BLOCK - 1::CMP_STRIDE]
    pos_cmp = jnp.pad(pos_cmp, ((0, 0), (0, n_cmp_rows - pos_cmp.shape[1])))[:, :, None]
    n_sel = S // SEL_BLOCK
    n_cmp = (S - CMP_BLOCK) // CMP_STRIDE + 1
    sel_start = np.arange(n_sel)[:, None] * SEL_BLOCK
    cmp_start = np.arange(n_cmp)[None, :] * CMP_STRIDE
    ov = np.clip(np.minimum(sel_start + SEL_BLOCK, cmp_start + CMP_BLOCK)
                 - np.maximum(sel_start, cmp_start), 0, None) / CMP_BLOCK
    ovt = np.zeros((n_sel, n_cmp_rows), np.float32)
    ovt[:, :n_cmp] = ov
    expand = np.zeros((N_KV_GROUPS, S // TK, LANES, TK), np.float32)
    keys = np.arange(S)
    for g in range(N_KV_GROUPS):
        expand[g, keys // TK, g * n_sel + keys // SEL_BLOCK, keys % TK] = 1.0
    ovt = jnp.asarray(ovt, BF16)
    expand = jnp.asarray(expand, BF16)

    for l in range(depth):
        w = w_in[l]
        wng = jnp.pad(w[:, o_ng:o_mg], ((0, 0), (0, LANES - (o_mg - o_ng))))
        q, kc_raw, vc_raw, ks, vs0, vs1, kw, vw0, vw1, gates, zp, gn = _in_proj(
            x, pos3, mix_norm_g[l][None, :],
            w[:, :o_q].astype(BF16), w[:, o_q:o_kv][:, perm].astype(BF16),
            w[:, o_kv:o_ng].astype(BF16), w[:, o_mg:].astype(BF16), wng.astype(BF16),
            inv_row, jnp.tile(q_norm_g[l], N_KV_GROUPS)[None, :],
            jnp.tile(k_norm_g[l], (1, N_KV_GROUPS)),
            pool_w[l].astype(BF16), pool_scale[l][None, :], w_pool_out[l].astype(BF16))
        pek, w1k, w2k = _compress_weights(cmp_pe[l, 0], cmp_w1[l, 0], cmp_w2[l, 0])
        pev, w1v, w2v = _compress_weights(cmp_pe[l, 1], cmp_w1[l, 1], cmp_w2[l, 1])
        attn = _nsa(q, kc_raw, vc_raw, ks, vs0, vs1, kw, vw0, vw1, gates, pos_cmp, pek, pev,
                    w1k, w1v, w2k, w2v, jnp.tile(k_norm_g[l, 0], N_KV_GROUPS)[None, :], inv_row,
                    ovt, expand)
        x = _ffn(x, zp, gn, attn, w_nsa_out[l][perm, :].astype(BF16), w_out[l].astype(BF16),
                 ffn_norm_g[l][None, :], w_up[l][:, :D_FF].astype(BF16),
                 w_up[l][:, D_FF:].astype(BF16), conv_w[l], conv_b[l][None, :],
                 w_down[l].astype(BF16))
    return x
```

```python
import functools

import numpy as np
import jax
import jax.numpy as jnp
from jax import lax
from jax.experimental import pallas as pl
from jax.experimental.pallas import tpu as pltpu

D_MODEL = 1024
POOL_WINDOWS = (2, 4, 8, 16)
POOL_GROUP_W = 128
POOL_W = len(POOL_WINDOWS) * POOL_GROUP_W
POOL_HALO = max(POOL_WINDOWS)
N_HEADS = 8
HEAD_DIM = 64
N_KV_GROUPS = 2
HEADS_PER_GROUP = N_HEADS // N_KV_GROUPS
ATTN_W = N_HEADS * HEAD_DIM
KV_W = N_KV_GROUPS * HEAD_DIM
N_KV_SLOTS = 6
N_NSA_BRANCHES = 3
CMP_BLOCK = 32
CMP_STRIDE = 16
CMP_HIDDEN = 128
SEL_BLOCK = 64
N_SELECT = 16
WINDOW = 512
ROPE_THETA = 500000.0
ROT_DIM = HEAD_DIM // 4
ROT_HALF = ROT_DIM // 2
D_FF = 2816
CONV_WIDTH = 3
RMS_EPS = 1e-6
NEG_INF = -1e30
SEL_FORCE = 1e9
LOG2_E = 1.4426950408889634

LANES = 128
SUBLANES = 8
VMEM_LIMIT = 56 * 1024 * 1024

TM_PROJ = 512
TQ = 128
TK = 512
TM_FFN = 512
FF_CHUNK = 256

BF16 = jnp.bfloat16
F32 = jnp.float32


def _lane_iota(shape):
    return lax.broadcasted_iota(jnp.int32, shape, len(shape) - 1)


def _row_iota(shape):
    return lax.broadcasted_iota(jnp.int32, shape, len(shape) - 2)


def _dot(a, b):
    return jnp.dot(a, b, preferred_element_type=F32)


def _dot_nt(a, b):
    return lax.dot_general(a, b, (((1,), (1,)), ((), ())), preferred_element_type=F32)


def _half_norm(v, gain_row):
    lo = _lane_iota(v.shape) < HEAD_DIM
    sq = v * v
    ss_lo = jnp.sum(jnp.where(lo, sq, 0.0), axis=-1, keepdims=True)
    ss_hi = jnp.sum(jnp.where(lo, 0.0, sq), axis=-1, keepdims=True)
    ms = jnp.where(lo, ss_lo, ss_hi) * (1.0 / HEAD_DIM)
    return v * lax.rsqrt(ms + RMS_EPS) * gain_row


def _rope_factors(pos_col, inv_row):
    ang = pos_col.astype(F32) * inv_row
    d = _lane_iota(ang.shape) & (HEAD_DIM - 1)
    cos = jnp.where(d < ROT_DIM, jnp.cos(ang), 1.0)
    sin = jnp.sin(ang)
    sin = jnp.where(d < ROT_HALF, -sin, jnp.where(d < ROT_DIM, sin, 0.0))
    return cos, sin


def _rope(v, cos, sin):
    d = _lane_iota(v.shape) & (HEAD_DIM - 1)
    partner = jnp.where(d < ROT_HALF,
                        pltpu.roll(v, LANES - ROT_HALF, 1),
                        pltpu.roll(v, ROT_HALF, 1))
    return v * cos + partner * sin


def _in_proj_kernel(x_ref, pos_ref, g_ref, wu_ref, wq_ref, wkv_ref, wmg_ref, wng_ref,
                    inv_ref, qg_ref, kg_ref, pw_ref, ps_ref, wpo_ref,
                    q_out, kc_out, vc_out, ks_out, vs_out, kw_out, vw_out,
                    gate_out, zp_out, gn_out, ext_ref):
    si = pl.program_id(1)
    tm = x_ref.shape[0]

    @pl.when(si == 0)
    def _():
        ext_ref[0:POOL_HALO, :] = jnp.zeros((POOL_HALO, POOL_W), F32)

    x = x_ref[...]
    ms = jnp.mean(x * x, axis=-1, keepdims=True)
    hb = (x * lax.rsqrt(ms + RMS_EPS) * g_ref[...]).astype(BF16)

    cos, sin = _rope_factors(pos_ref[...], inv_ref[...])
    q = _dot(hb, wq_ref[...])
    scale = HEAD_DIM ** -0.5 * LOG2_E
    for j in range(ATTN_W // LANES):
        sl = slice(j * LANES, (j + 1) * LANES)
        qn = _rope(_half_norm(q[:, sl], qg_ref[...]), cos, sin)
        q_out[:, sl] = (qn * scale).astype(BF16)

    kv = _dot(hb, wkv_ref[...])
    kc_out[...] = kv[:, 0 * KV_W:1 * KV_W].astype(BF16)
    vc_out[...] = kv[:, 1 * KV_W:2 * KV_W].astype(BF16)
    ks = _half_norm(kv[:, 2 * KV_W:3 * KV_W], kg_ref[1:2, :])
    ks_out[:, :KV_W] = _rope(ks, cos, sin).astype(BF16)
    tok_blk = (si * tm + _row_iota((tm, LANES))) // SEL_BLOCK
    ks_out[:, KV_W:] = jnp.where(_lane_iota((tm, LANES)) == tok_blk, NEG_INF, 0.0).astype(BF16)
    _store_value_ext(kv[:, 3 * KV_W:4 * KV_W], vs_out)
    kw = _half_norm(kv[:, 4 * KV_W:5 * KV_W], kg_ref[2:3, :])
    kw_out[...] = _rope(kw, cos, sin).astype(BF16)
    _store_value_ext(kv[:, 5 * KV_W:6 * KV_W], vw_out)

    u = _dot(hb, wu_ref[...])
    ext_ref[POOL_HALO:, :] = u
    t = si * tm + _row_iota((tm, 1))
    count = (t + 1).astype(F32)
    pooled = []
    for gi, w in enumerate(POOL_WINDOWS):
        sl = slice(gi * POOL_GROUP_W, (gi + 1) * POOL_GROUP_W)
        acc = ext_ref[:, sl]
        step = 1
        while step < w:
            acc = acc + pltpu.roll(acc, step, 0)
            step *= 2
        win_sum = acc[POOL_HALO:, :]
        mean = win_sum / jnp.minimum(count, float(w))
        pm = (mean - u[:, sl]).astype(BF16)
        pooled.append((_dot(pm, pw_ref[gi]) * ps_ref[:, sl]).astype(BF16))
    ext_ref[0:POOL_HALO, :] = ext_ref[tm:tm + POOL_HALO, :]
    y_pool = _dot(jnp.concatenate(pooled, axis=-1), wpo_ref[...])

    mg = _dot(hb, wmg_ref[...])
    zp_out[...] = (jax.nn.sigmoid(mg[:, :D_MODEL]) * y_pool).astype(BF16)
    gn_out[...] = jax.nn.sigmoid(mg[:, D_MODEL:]).astype(BF16)
    gate_out[...] = jax.nn.sigmoid(_dot(hb, wng_ref[...]))


def _store_value_ext(v, out):
    lo = _lane_iota(v.shape) < HEAD_DIM
    out[:, :KV_W] = jnp.where(lo, v, 1.0).astype(BF16)
    out[:, KV_W:] = jnp.where(lo, 1.0, v).astype(BF16)


def _const_spec(shape):
    nd = len(shape)
    return pl.BlockSpec(shape, lambda b, s: (0,) * nd, pipeline_mode=pl.Buffered(1))


def _in_proj(x, pos3, g_row, wu, wq, wkv, wmg, wng, inv_row, qg_row, kg_rows, pw, ps_row, wpo):
    B, S, _ = x.shape
    tm = TM_PROJ
    tok = lambda w: pl.BlockSpec((None, tm, w), lambda b, s: (b, s, 0))
    out_shapes = (
        jax.ShapeDtypeStruct((B, S, ATTN_W), BF16),
        jax.ShapeDtypeStruct((B, S, KV_W), BF16),
        jax.ShapeDtypeStruct((B, S, KV_W), BF16),
        jax.ShapeDtypeStruct((B, S, 2 * KV_W), BF16),
        jax.ShapeDtypeStruct((B, S, 2 * KV_W), BF16),
        jax.ShapeDtypeStruct((B, S, KV_W), BF16),
        jax.ShapeDtypeStruct((B, S, 2 * KV_W), BF16),
        jax.ShapeDtypeStruct((B, S, LANES), F32),
        jax.ShapeDtypeStruct((B, S, D_MODEL), BF16),
        jax.ShapeDtypeStruct((B, S, D_MODEL), BF16),
    )
    out_specs = (tok(ATTN_W), tok(KV_W), tok(KV_W), tok(2 * KV_W), tok(2 * KV_W), tok(KV_W),
                 tok(2 * KV_W), tok(LANES), tok(D_MODEL), tok(D_MODEL))
    in_specs = [
        tok(D_MODEL),
        pl.BlockSpec((None, tm, 1), lambda b, s: (b, s, 0)),
        _const_spec(g_row.shape), _const_spec(wu.shape), _const_spec(wq.shape),
        _const_spec(wkv.shape), _const_spec(wmg.shape), _const_spec(wng.shape),
        _const_spec(inv_row.shape), _const_spec(qg_row.shape), _const_spec(kg_rows.shape),
        _const_spec(pw.shape), _const_spec(ps_row.shape), _const_spec(wpo.shape),
    ]
    return pl.pallas_call(
        _in_proj_kernel,
        grid=(B, S // tm),
        in_specs=in_specs,
        out_specs=out_specs,
        out_shape=out_shapes,
        scratch_shapes=[pltpu.VMEM((POOL_HALO + tm, POOL_W), F32)],
        compiler_params=pltpu.CompilerParams(
            dimension_semantics=("arbitrary", "arbitrary"),
            vmem_limit_bytes=VMEM_LIMIT),
        name="in_proj",
    )(x, pos3, g_row, wu, wq, wkv, wmg, wng, inv_row, qg_row, kg_rows, pw, ps_row, wpo)


def _compress(raw_ref, pe_ref, w1_ref, w2_ref):
    a = raw_ref[...].astype(F32)
    first = _dot((a + pe_ref[0:1, :]).astype(BF16), w1_ref[0])
    second = _dot((a + pe_ref[1:2, :]).astype(BF16), w1_ref[1])
    nblk = first.shape[0]
    hid = first + pltpu.roll(second, nblk - 1, 0)
    hid = hid * jax.nn.sigmoid(hid)
    return _dot(hid.astype(BF16), w2_ref[...])


def _normalize(acc):
    return acc / pltpu.roll(acc, HEAD_DIM, 1)


def _group_halves(pv):
    grp_rows = pv.shape[0] // N_KV_GROUPS
    return jnp.concatenate([pv[g * grp_rows:(g + 1) * grp_rows, g * KV_W:(g + 1) * KV_W]
                            for g in range(N_KV_GROUPS)], axis=0)


def _nsa_kernel(q_ref, kcr_ref, vcr_ref, ks_ref, vs_ref, kw_ref, vw_ref, gate_ref, posc_ref,
                pek_ref, pev_ref, w1k_ref, w1v_ref, w2k_ref, w2v_ref, kg_ref, inv_ref, ovt_ref,
                o_ref, kc_s, vc_s, m_ref, acc_ref):
    qi = pl.program_id(1)
    seq = ks_ref.shape[0]
    grp_rows = HEADS_PER_GROUP * TQ

    @pl.when(qi == 0)
    def _():
        kc = _compress(kcr_ref, pek_ref, w1k_ref, w2k_ref)
        cos, sin = _rope_factors(posc_ref[...], inv_ref[...])
        kc_s[...] = _rope(_half_norm(kc, kg_ref[...]), cos, sin).astype(BF16)
        vc_s[...] = _compress(vcr_ref, pev_ref, w1v_ref, w2v_ref).astype(BF16)

    t0 = qi * TQ
    t = t0 + _row_iota((TQ, 1))
    lane = _lane_iota((TQ, LANES))
    lo = lane < HEAD_DIM

    q = q_ref[...]
    zero = jnp.zeros((TQ, LANES), BF16)
    slabs = [q[:, j * LANES:(j + 1) * LANES] for j in range(HEADS_PER_GROUP)]
    qs = jnp.concatenate([jnp.where(lo, s, zero) for s in slabs]
                         + [jnp.where(lo, zero, s) for s in slabs], axis=0)
    rows = N_HEADS * TQ

    s = _dot_nt(qs, kc_s[...])
    vis = lane * CMP_STRIDE + (CMP_BLOCK - 1) <= t
    s = (s.reshape(N_HEADS, TQ, LANES) + jnp.where(vis, 0.0, NEG_INF)[None]).reshape(rows, LANES)
    p = jnp.exp2(s - jnp.max(s, axis=-1, keepdims=True))
    ones = jnp.ones((LANES, LANES), BF16)
    p_hi = p.astype(BF16)
    p_lo = (p - p_hi.astype(F32)).astype(BF16)
    p = p / (_dot(p_hi, ones) + _dot(p_lo, ones))
    p = p.reshape(N_HEADS, TQ, LANES)
    p = jnp.where((t >= CMP_BLOCK - 1)[None], p, 0.0)
    o_cmp = _dot(p.reshape(rows, LANES).astype(BF16), vc_s[...])

    n_win = WINDOW + TQ
    start = pl.multiple_of(jnp.maximum(t0 - WINDOW, 0), TQ)
    dist = t - (start + _lane_iota((TQ, n_win)))
    bias = jnp.where((dist >= 0) & (dist < WINDOW), 0.0, NEG_INF)
    s = _dot_nt(qs, kw_ref[pl.ds(start, n_win), :])
    s = (s.reshape(N_HEADS, TQ, n_win) + bias[None]).reshape(rows, n_win)
    pw = jnp.exp2(s - jnp.max(s, axis=-1, keepdims=True)).astype(BF16)
    o_win = _normalize(_group_halves(_dot(pw, vw_ref[pl.ds(start, n_win), :])))

    gates = gate_ref[...]
    part = [gates[:, N_NSA_BRANCHES * h:N_NSA_BRANCHES * h + 1] * o_cmp[h * TQ:(h + 1) * TQ]
            + gates[:, N_NSA_BRANCHES * h + 2:N_NSA_BRANCHES * h + 3] * o_win[h * TQ:(h + 1) * TQ]
            for h in range(N_HEADS)]

    n_sel = seq // SEL_BLOCK
    blk = _row_iota((n_sel, TQ))
    tq_lane = t0 + _lane_iota((n_sel, TQ))
    cur = tq_lane // SEL_BLOCK
    forced = (blk == 0) | (blk == cur) | (blk == cur - 1)
    future = blk > cur
    eye = jnp.where(_row_iota((TQ, TQ)) == _lane_iota((TQ, TQ)), 1.0, 0.0).astype(BF16)
    pad = jnp.zeros((LANES - n_sel, TQ), F32)
    q_wide = []
    for g in range(N_KV_GROUPS):
        psum = jnp.sum(p[g * HEADS_PER_GROUP:(g + 1) * HEADS_PER_GROUP], axis=0)
        ps_hi = psum.astype(BF16)
        ps_lo = (psum - ps_hi.astype(F32)).astype(BF16)
        imp = _dot_nt(ovt_ref[...], ps_hi) + _dot_nt(ovt_ref[...], ps_lo)
        val = jnp.where(forced, SEL_FORCE, jnp.where(future, NEG_INF, imp))
        rank = jnp.zeros((n_sel, TQ), jnp.int32)
        for d in range(1, n_sel):
            other = pltpu.roll(val, d, 0)
            ahead = (other > val) | ((other == val) & (blk >= d))
            rank = rank + ahead.astype(jnp.int32)
        dropped = jnp.concatenate([jnp.where(rank < N_SELECT, 0.0, 1.0), pad], axis=0)
        dropped = _dot_nt(eye, dropped.astype(BF16)).astype(BF16)
        r = slice(g * grp_rows, (g + 1) * grp_rows)
        q_wide.append(jnp.concatenate([qs[r], jnp.tile(dropped, (HEADS_PER_GROUP, 1))], axis=1))
    q_wide = jnp.concatenate(q_wide, axis=0)

    m_ref[...] = jnp.full(m_ref.shape, NEG_INF, F32)
    acc_ref[...] = jnp.zeros(acc_ref.shape, F32)

    def sel_tile(kt, causal):
        start = pl.multiple_of(kt * TK, TK)
        s = _dot_nt(q_wide, ks_ref[pl.ds(start, TK), :])
        if causal:
            bias = jnp.where(start + _lane_iota((TQ, TK)) <= t, 0.0, NEG_INF)
            s = (s.reshape(N_HEADS, TQ, TK) + bias[None]).reshape(rows, TK)
        m_old = m_ref[...]
        m_new = jnp.maximum(m_old, jnp.max(s, axis=-1, keepdims=True))
        p = jnp.exp2(s - jnp.tile(m_new, (1, TK // LANES))).astype(BF16)
        pv = _group_halves(_dot(p, vs_ref[pl.ds(start, TK), :]))
        acc_ref[...] = jnp.exp2(m_old - m_new) * acc_ref[...] + pv
        m_ref[...] = m_new

    last = (t0 + TQ - 1) // TK
    lax.fori_loop(0, last, lambda kt, carry: (sel_tile(kt, False), carry)[1], 0)
    sel_tile(last, True)
    o_sel = _normalize(acc_ref[...])

    heads = [part[h] + gates[:, N_NSA_BRANCHES * h + 1:N_NSA_BRANCHES * h + 2]
             * o_sel[h * TQ:(h + 1) * TQ] for h in range(N_HEADS)]
    for j in range(HEADS_PER_GROUP):
        o_ref[:, j * LANES:(j + 1) * LANES] = jnp.where(
            lo, heads[j], heads[HEADS_PER_GROUP + j]).astype(BF16)


def _nsa(q, kc_raw, vc_raw, ks, vs, kw, vw, gates, pos_cmp, pek, pev, w1k, w1v, w2k, w2v,
         kg_row, inv_row, ovt):
    B, S, _ = q.shape
    assert S % TK == 0 and S >= WINDOW + TQ and LANES >= S // SEL_BLOCK
    n_rows = S // CMP_STRIDE
    kv_spec = lambda w: pl.BlockSpec((None, S, w), lambda b, i: (b, 0, 0))
    raw_spec = pl.BlockSpec((None, n_rows, CMP_STRIDE * KV_W), lambda b, i: (b, 0, 0))
    consts = (pek, pev, w1k, w1v, w2k, w2v, kg_row, inv_row, ovt)
    in_specs = [
        pl.BlockSpec((None, TQ, ATTN_W), lambda b, i: (b, i, 0)),
        raw_spec, raw_spec, kv_spec(2 * KV_W), kv_spec(2 * KV_W), kv_spec(KV_W), kv_spec(2 * KV_W),
        pl.BlockSpec((None, TQ, LANES), lambda b, i: (b, i, 0)),
        pl.BlockSpec((None, n_rows, 1), lambda b, i: (b, 0, 0)),
    ] + [_const_spec(c.shape) for c in consts]
    rows = N_HEADS * TQ
    return pl.pallas_call(
        _nsa_kernel,
        grid=(B, S // TQ),
        in_specs=in_specs,
        out_specs=pl.BlockSpec((None, TQ, ATTN_W), lambda b, i: (b, i, 0)),
        out_shape=jax.ShapeDtypeStruct((B, S, ATTN_W), BF16),
        scratch_shapes=[
            pltpu.VMEM((n_rows, KV_W), BF16),
            pltpu.VMEM((n_rows, KV_W), BF16),
            pltpu.VMEM((rows, LANES), F32),
            pltpu.VMEM((rows, LANES), F32),
        ],
        compiler_params=pltpu.CompilerParams(
            dimension_semantics=("arbitrary", "arbitrary"),
            vmem_limit_bytes=VMEM_LIMIT),
        name="nsa",
    )(q, kc_raw.reshape(B, n_rows, CMP_STRIDE * KV_W), vc_raw.reshape(B, n_rows, CMP_STRIDE * KV_W),
      ks, vs, kw, vw, gates, pos_cmp, *consts)


def _ffn_kernel(x_ref, zp_ref, gn_ref, a_ref, wno_ref, wo_ref, g_ref, wug_ref, wuv_ref,
                cw_ref, cb_ref, wd_ref, o_ref, carry_ref, act_ref):
    si = pl.program_id(1)
    tm = x_ref.shape[0]

    @pl.when(si == 0)
    def _():
        carry_ref[...] = jnp.zeros(carry_ref.shape, F32)

    y_nsa = _dot(a_ref[...], wno_ref[...])
    merged = zp_ref[...].astype(F32) + gn_ref[...].astype(F32) * y_nsa
    x1 = x_ref[...] + _dot(merged.astype(BF16), wo_ref[...])
    ms = jnp.mean(x1 * x1, axis=-1, keepdims=True)
    hb = (x1 * lax.rsqrt(ms + RMS_EPS) * g_ref[...]).astype(BF16)

    row = _row_iota((tm, FF_CHUNK))
    for c in range(D_FF // FF_CHUNK):
        sl = slice(c * FF_CHUNK, (c + 1) * FF_CHUNK)
        gp = _dot(hb, wug_ref[:, sl])
        val = _dot(hb, wuv_ref[:, sl])
        prev = carry_ref[:, sl]
        p1 = prev[SUBLANES - 1:SUBLANES, :]
        p2 = prev[SUBLANES - 2:SUBLANES - 1, :]
        lag1 = jnp.where(row == 0, p1, pltpu.roll(gp, 1, 0))
        lag2 = jnp.where(row == 0, p2, jnp.where(row == 1, p1, pltpu.roll(gp, 2, 0)))
        carry_ref[:, sl] = gp[tm - SUBLANES:, :]
        gc = (cw_ref[0:1, sl] * lag2 + cw_ref[1:2, sl] * lag1 + cw_ref[2:3, sl] * gp
              + cb_ref[:, sl])
        act_ref[:, sl] = (gc * jax.nn.sigmoid(gc) * val).astype(BF16)
    o_ref[...] = x1 + _dot(act_ref[...], wd_ref[...])


def _ffn(x, zp, gn, attn, wno, wo, g_row, wug, wuv, cw, cb_row, wd):
    B, S, _ = x.shape
    tm = TM_FFN
    tok = lambda w: pl.BlockSpec((None, tm, w), lambda b, s: (b, s, 0))
    consts = (wno, wo, g_row, wug, wuv, cw, cb_row, wd)
    return pl.pallas_call(
        _ffn_kernel,
        grid=(B, S // tm),
        in_specs=[tok(D_MODEL), tok(D_MODEL), tok(D_MODEL), tok(ATTN_W)]
                 + [_const_spec(c.shape) for c in consts],
        out_specs=tok(D_MODEL),
        out_shape=jax.ShapeDtypeStruct((B, S, D_MODEL), F32),
        scratch_shapes=[pltpu.VMEM((SUBLANES, D_FF), F32),
                        pltpu.VMEM((tm, D_FF), BF16)],
        compiler_params=pltpu.CompilerParams(
            dimension_semantics=("arbitrary", "arbitrary"),
            vmem_limit_bytes=VMEM_LIMIT),
        name="merge_ffn",
    )(x, zp, gn, attn, *consts)


def _slab_perm():
    j, half, d = np.meshgrid(np.arange(HEADS_PER_GROUP), np.arange(N_KV_GROUPS),
                             np.arange(HEAD_DIM), indexing="ij")
    return ((j + HEADS_PER_GROUP * half) * HEAD_DIM + d).reshape(-1)


def _group_block_diag(w):
    z = jnp.zeros_like(w)
    return jnp.concatenate([jnp.concatenate([w, z], axis=-1),
                            jnp.concatenate([z, w], axis=-1)], axis=-2)


def _compress_weights(pe, w1, w2):
    half = CMP_BLOCK // 2
    pe_rows = jnp.tile(pe.reshape(2, half, 1, HEAD_DIM), (1, 1, N_KV_GROUPS, 1)).reshape(2, -1)
    w1h = w1.reshape(2, half, HEAD_DIM, CMP_HIDDEN)
    w1bd = _group_block_diag(w1h).reshape(2, half * KV_W, N_KV_GROUPS * CMP_HIDDEN)
    return pe_rows.astype(F32), w1bd.astype(BF16), _group_block_diag(w2).astype(BF16)


def kernel(x, positions, mix_norm_g, w_in, q_norm_g, k_norm_g, cmp_pe, cmp_w1, cmp_w2, pool_w,
           pool_scale, w_pool_out, w_nsa_out, w_out, ffn_norm_g, w_up, conv_w, conv_b, w_down):
    B, S, _ = x.shape
    depth = w_in.shape[0]
    perm = _slab_perm()
    o_q = POOL_W
    o_kv = o_q + ATTN_W
    o_ng = o_kv + N_KV_SLOTS * KV_W
    o_mg = o_ng + N_HEADS * N_NSA_BRANCHES
    half = jnp.arange(ROT_HALF, dtype=F32)
    inv = ROPE_THETA ** (-(half * 2.0 / ROT_DIM))
    inv_row = jnp.tile(inv, LANES // ROT_HALF)[None, :]
    pos3 = positions[:, :, None]
    n_cmp_rows = S // CMP_STRIDE
    pos_cmp = positions[:, CMP_BLOCK - 1::CMP_STRIDE]
    pos_cmp = jnp.pad(pos_cmp, ((0, 0), (0, n_cmp_rows - pos_cmp.shape[1])))[:, :, None]
    n_sel = S // SEL_BLOCK
    n_cmp = (S - CMP_BLOCK) // CMP_STRIDE + 1
    sel_start = np.arange(n_sel)[:, None] * SEL_BLOCK
    cmp_start = np.arange(n_cmp)[None, :] * CMP_STRIDE
    ov = np.clip(np.minimum(sel_start + SEL_BLOCK, cmp_start + CMP_BLOCK)
                 - np.maximum(sel_start, cmp_start), 0, None) / CMP_BLOCK
    ovt = np.zeros((n_sel, n_cmp_rows), np.float32)
    ovt[:, :n_cmp] = ov
    ovt = jnp.asarray(ovt, BF16)

    for l in range(depth):
        w = w_in[l]
        wng = jnp.pad(w[:, o_ng:o_mg], ((0, 0), (0, LANES - (o_mg - o_ng))))
        q, kc_raw, vc_raw, ks, vs, kw, vw, gates, zp, gn = _in_proj(
            x, pos3, mix_norm_g[l][None, :],
            w[:, :o_q].astype(BF16), w[:, o_q:o_kv][:, perm].astype(BF16),
            w[:, o_kv:o_ng].astype(BF16), w[:, o_mg:].astype(BF16), wng.astype(BF16),
            inv_row, jnp.tile(q_norm_g[l], N_KV_GROUPS)[None, :],
            jnp.tile(k_norm_g[l], (1, N_KV_GROUPS)),
            pool_w[l].astype(BF16), pool_scale[l][None, :], w_pool_out[l].astype(BF16))
        pek, w1k, w2k = _compress_weights(cmp_pe[l, 0], cmp_w1[l, 0], cmp_w2[l, 0])
        pev, w1v, w2v = _compress_weights(cmp_pe[l, 1], cmp_w1[l, 1], cmp_w2[l, 1])
        attn = _nsa(q, kc_raw, vc_raw, ks, vs, kw, vw, gates, pos_cmp, pek, pev, w1k, w1v,
                    w2k, w2v, jnp.tile(k_norm_g[l, 0], N_KV_GROUPS)[None, :], inv_row, ovt)
        x = _ffn(x, zp, gn, attn, w_nsa_out[l][perm, :].astype(BF16), w_out[l].astype(BF16),
                 ffn_norm_g[l][None, :], w_up[l][:, :D_FF].astype(BF16),
                 w_up[l][:, D_FF:].astype(BF16), conv_w[l], conv_b[l][None, :],
                 w_down[l].astype(BF16))
    return x
```

```python
import functools

import numpy as np
import jax
import jax.numpy as jnp
from jax import lax
from jax.experimental import pallas as pl
from jax.experimental.pallas import tpu as pltpu

D_MODEL = 1024
POOL_WINDOWS = (2, 4, 8, 16)
POOL_GROUP_W = 128
POOL_W = len(POOL_WINDOWS) * POOL_GROUP_W
POOL_HALO = max(POOL_WINDOWS)
N_HEADS = 8
HEAD_DIM = 64
N_KV_GROUPS = 2
HEADS_PER_GROUP = N_HEADS // N_KV_GROUPS
ATTN_W = N_HEADS * HEAD_DIM
KV_W = N_KV_GROUPS * HEAD_DIM
N_KV_SLOTS = 6
N_NSA_BRANCHES = 3
CMP_BLOCK = 32
CMP_STRIDE = 16
CMP_HIDDEN = 128
SEL_BLOCK = 64
N_SELECT = 16
WINDOW = 512
ROPE_THETA = 500000.0
ROT_DIM = HEAD_DIM // 4
ROT_HALF = ROT_DIM // 2
D_FF = 2816
CONV_WIDTH = 3
RMS_EPS = 1e-6
NEG_INF = -1e30
SEL_FORCE = 1e9
LOG2_E = 1.4426950408889634

LANES = 128
SUBLANES = 8
VMEM_LIMIT = 56 * 1024 * 1024

TM_PROJ = 512
TQ = 256
TK = 512
TM_FFN = 512
FF_CHUNK = 256

BF16 = jnp.bfloat16
F32 = jnp.float32


def _lane_iota(shape):
    return lax.broadcasted_iota(jnp.int32, shape, len(shape) - 1)


def _row_iota(shape):
    return lax.broadcasted_iota(jnp.int32, shape, len(shape) - 2)


def _dot(a, b):
    return jnp.dot(a, b, preferred_element_type=F32)


def _dot_nt(a, b):
    return lax.dot_general(a, b, (((1,), (1,)), ((), ())), preferred_element_type=F32)


def _half_norm(v, gain_row):
    lo = _lane_iota(v.shape) < HEAD_DIM
    sq = v * v
    ss_lo = jnp.sum(jnp.where(lo, sq, 0.0), axis=-1, keepdims=True)
    ss_hi = jnp.sum(jnp.where(lo, 0.0, sq), axis=-1, keepdims=True)
    ms = jnp.where(lo, ss_lo, ss_hi) * (1.0 / HEAD_DIM)
    return v * lax.rsqrt(ms + RMS_EPS) * gain_row


def _rope_factors(pos_col, inv_row):
    ang = pos_col.astype(F32) * inv_row
    d = _lane_iota(ang.shape) & (HEAD_DIM - 1)
    cos = jnp.where(d < ROT_DIM, jnp.cos(ang), 1.0)
    sin = jnp.sin(ang)
    sin = jnp.where(d < ROT_HALF, -sin, jnp.where(d < ROT_DIM, sin, 0.0))
    return cos, sin


def _rope(v, cos, sin):
    d = _lane_iota(v.shape) & (HEAD_DIM - 1)
    partner = jnp.where(d < ROT_HALF,
                        pltpu.roll(v, LANES - ROT_HALF, 1),
                        pltpu.roll(v, ROT_HALF, 1))
    return v * cos + partner * sin


def _in_proj_kernel(x_ref, pos_ref, g_ref, wu_ref, wq_ref, wkv_ref, wmg_ref, wng_ref,
                    inv_ref, qg_ref, kg_ref, pw_ref, ps_ref, wpo_ref,
                    q_out, kc_out, vc_out, ks_out, vs_out, kw_out, vw_out,
                    gate_out, zp_out, gn_out, ext_ref):
    si = pl.program_id(1)
    tm = x_ref.shape[0]

    @pl.when(si == 0)
    def _():
        ext_ref[0:POOL_HALO, :] = jnp.zeros((POOL_HALO, POOL_W), F32)

    x = x_ref[...]
    ms = jnp.mean(x * x, axis=-1, keepdims=True)
    hb = (x * lax.rsqrt(ms + RMS_EPS) * g_ref[...]).astype(BF16)

    cos, sin = _rope_factors(pos_ref[...], inv_ref[...])
    q = _dot(hb, wq_ref[...])
    scale = HEAD_DIM ** -0.5 * LOG2_E
    for j in range(ATTN_W // LANES):
        sl = slice(j * LANES, (j + 1) * LANES)
        qn = _rope(_half_norm(q[:, sl], qg_ref[...]), cos, sin)
        q_out[:, sl] = (qn * scale).astype(BF16)

    kv = _dot(hb, wkv_ref[...])
    kc_out[...] = kv[:, 0 * KV_W:1 * KV_W].astype(BF16)
    vc_out[...] = kv[:, 1 * KV_W:2 * KV_W].astype(BF16)
    ks = _half_norm(kv[:, 2 * KV_W:3 * KV_W], kg_ref[1:2, :])
    ks_out[:, :KV_W] = _rope(ks, cos, sin).astype(BF16)
    tok_blk = (si * tm + _row_iota((tm, LANES))) // SEL_BLOCK
    ks_out[:, KV_W:] = jnp.where(_lane_iota((tm, LANES)) == tok_blk, NEG_INF, 0.0).astype(BF16)
    v_sel = kv[:, 3 * KV_W:4 * KV_W]
    lo = _lane_iota(v_sel.shape) < HEAD_DIM
    vs_out[:, :KV_W] = jnp.where(lo, v_sel, 1.0).astype(BF16)
    vs_out[:, KV_W:] = jnp.where(lo, 1.0, v_sel).astype(BF16)
    kw = _half_norm(kv[:, 4 * KV_W:5 * KV_W], kg_ref[2:3, :])
    kw_out[...] = _rope(kw, cos, sin).astype(BF16)
    _store_value_ext(kv[:, 5 * KV_W:6 * KV_W], vw_out)

    u = _dot(hb, wu_ref[...])
    ext_ref[POOL_HALO:, :] = u
    t = si * tm + _row_iota((tm, 1))
    count = (t + 1).astype(F32)
    pooled = []
    for gi, w in enumerate(POOL_WINDOWS):
        sl = slice(gi * POOL_GROUP_W, (gi + 1) * POOL_GROUP_W)
        acc = ext_ref[:, sl]
        step = 1
        while step < w:
            acc = acc + pltpu.roll(acc, step, 0)
            step *= 2
        win_sum = acc[POOL_HALO:, :]
        mean = win_sum / jnp.minimum(count, float(w))
        pm = (mean - u[:, sl]).astype(BF16)
        pooled.append((_dot(pm, pw_ref[gi]) * ps_ref[:, sl]).astype(BF16))
    ext_ref[0:POOL_HALO, :] = ext_ref[tm:tm + POOL_HALO, :]
    y_pool = _dot(jnp.concatenate(pooled, axis=-1), wpo_ref[...])

    mg = _dot(hb, wmg_ref[...])
    zp_out[...] = (jax.nn.sigmoid(mg[:, :D_MODEL]) * y_pool).astype(BF16)
    gn_out[...] = jax.nn.sigmoid(mg[:, D_MODEL:]).astype(BF16)
    gate_out[...] = jax.nn.sigmoid(_dot(hb, wng_ref[...]))


def _store_value_ext(v, out):
    out[:, :KV_W] = v.astype(BF16)
    out[:, KV_W:] = jnp.ones(v.shape, BF16)


def _const_spec(shape):
    nd = len(shape)
    return pl.BlockSpec(shape, lambda b, s: (0,) * nd, pipeline_mode=pl.Buffered(1))


def _in_proj(x, pos3, g_row, wu, wq, wkv, wmg, wng, inv_row, qg_row, kg_rows, pw, ps_row, wpo):
    B, S, _ = x.shape
    tm = TM_PROJ
    tok = lambda w: pl.BlockSpec((None, tm, w), lambda b, s: (b, s, 0))
    out_shapes = (
        jax.ShapeDtypeStruct((B, S, ATTN_W), BF16),
        jax.ShapeDtypeStruct((B, S, KV_W), BF16),
        jax.ShapeDtypeStruct((B, S, KV_W), BF16),
        jax.ShapeDtypeStruct((B, S, 2 * KV_W), BF16),
        jax.ShapeDtypeStruct((B, S, 2 * KV_W), BF16),
        jax.ShapeDtypeStruct((B, S, KV_W), BF16),
        jax.ShapeDtypeStruct((B, S, 2 * KV_W), BF16),
        jax.ShapeDtypeStruct((B, S, LANES), F32),
        jax.ShapeDtypeStruct((B, S, D_MODEL), BF16),
        jax.ShapeDtypeStruct((B, S, D_MODEL), BF16),
    )
    out_specs = (tok(ATTN_W), tok(KV_W), tok(KV_W), tok(2 * KV_W), tok(2 * KV_W), tok(KV_W),
                 tok(2 * KV_W), tok(LANES), tok(D_MODEL), tok(D_MODEL))
    in_specs = [
        tok(D_MODEL),
        pl.BlockSpec((None, tm, 1), lambda b, s: (b, s, 0)),
        _const_spec(g_row.shape), _const_spec(wu.shape), _const_spec(wq.shape),
        _const_spec(wkv.shape), _const_spec(wmg.shape), _const_spec(wng.shape),
        _const_spec(inv_row.shape), _const_spec(qg_row.shape), _const_spec(kg_rows.shape),
        _const_spec(pw.shape), _const_spec(ps_row.shape), _const_spec(wpo.shape),
    ]
    return pl.pallas_call(
        _in_proj_kernel,
        grid=(B, S // tm),
        in_specs=in_specs,
        out_specs=out_specs,
        out_shape=out_shapes,
        scratch_shapes=[pltpu.VMEM((POOL_HALO + tm, POOL_W), F32)],
        compiler_params=pltpu.CompilerParams(
            dimension_semantics=("arbitrary", "arbitrary"),
            vmem_limit_bytes=VMEM_LIMIT),
        name="in_proj",
    )(x, pos3, g_row, wu, wq, wkv, wmg, wng, inv_row, qg_row, kg_rows, pw, ps_row, wpo)


def _compress(raw_ref, pe_ref, w1_ref, w2_ref):
    a = raw_ref[...].astype(F32)
    first = _dot((a + pe_ref[0:1, :]).astype(BF16), w1_ref[0])
    second = _dot((a + pe_ref[1:2, :]).astype(BF16), w1_ref[1])
    nblk = first.shape[0]
    hid = first + pltpu.roll(second, nblk - 1, 0)
    hid = hid * jax.nn.sigmoid(hid)
    return _dot(hid.astype(BF16), w2_ref[...])


def _gated(gates, branch, o):
    out = []
    for h in range(N_HEADS):
        c = N_NSA_BRANCHES * h + branch
        out.append(gates[:, c:c + 1] * o[h * TQ:(h + 1) * TQ])
    return out


def _group_halves(pv):
    grp_rows = pv.shape[0] // N_KV_GROUPS
    return jnp.concatenate([pv[g * grp_rows:(g + 1) * grp_rows, g * KV_W:(g + 1) * KV_W]
                            for g in range(N_KV_GROUPS)], axis=0)


def _nsa_kernel(q_ref, kcr_ref, vcr_ref, ks_ref, vs_ref, kw_ref, vw_ref, gate_ref, posc_ref,
                pek_ref, pev_ref, w1k_ref, w1v_ref, w2k_ref, w2v_ref, kg_ref, inv_ref, ovt_ref,
                o_ref, kc_s, vc_s, m_ref, acc_ref):
    qi = pl.program_id(1)
    seq = ks_ref.shape[0]
    grp_rows = HEADS_PER_GROUP * TQ

    @pl.when(qi == 0)
    def _():
        kc = _compress(kcr_ref, pek_ref, w1k_ref, w2k_ref)
        cos, sin = _rope_factors(posc_ref[...], inv_ref[...])
        kc_s[...] = _rope(_half_norm(kc, kg_ref[...]), cos, sin).astype(BF16)
        _store_value_ext(_compress(vcr_ref, pev_ref, w1v_ref, w2v_ref), vc_s)

    t0 = qi * TQ
    t = t0 + _row_iota((TQ, 1))
    lane = _lane_iota((TQ, LANES))
    lo = lane < HEAD_DIM

    q = q_ref[...]
    zero = jnp.zeros((TQ, LANES), BF16)
    slabs = [q[:, j * LANES:(j + 1) * LANES] for j in range(HEADS_PER_GROUP)]
    qs = jnp.concatenate([jnp.where(lo, s, zero) for s in slabs]
                         + [jnp.where(lo, zero, s) for s in slabs], axis=0)
    rows = N_HEADS * TQ

    s = _dot_nt(qs, kc_s[...])
    vis = lane * CMP_STRIDE + (CMP_BLOCK - 1) <= t
    s = (s.reshape(N_HEADS, TQ, LANES) + jnp.where(vis, 0.0, NEG_INF)[None]).reshape(rows, LANES)
    p = jnp.exp2(s - jnp.max(s, axis=-1, keepdims=True))
    pv = _dot(p.astype(BF16), vc_s[...])
    any_visible = t >= CMP_BLOCK - 1
    inv_l = 1.0 / pv[:, KV_W:]
    gates = gate_ref[...]
    part = [jnp.where(any_visible, o, 0.0) for o in _gated(gates, 0, pv[:, :KV_W] * inv_l)]
    p = (p * inv_l).reshape(N_HEADS, TQ, LANES)
    p = jnp.where(any_visible[None], p, 0.0)

    n_win = WINDOW + TQ
    start = pl.multiple_of(jnp.maximum(t0 - WINDOW, 0), TQ)
    dist = t - (start + _lane_iota((TQ, n_win)))
    bias = jnp.where((dist >= 0) & (dist < WINDOW), 0.0, NEG_INF)
    s = _dot_nt(qs, kw_ref[pl.ds(start, n_win), :])
    s = (s.reshape(N_HEADS, TQ, n_win) + bias[None]).reshape(rows, n_win)
    pw = jnp.exp2(s - jnp.max(s, axis=-1, keepdims=True)).astype(BF16)
    pv = _dot(pw, vw_ref[pl.ds(start, n_win), :])
    part = [a + b for a, b in zip(part, _gated(gates, 2, pv[:, :KV_W] / pv[:, KV_W:]))]

    n_sel = seq // SEL_BLOCK
    blk = _row_iota((n_sel, TQ))
    tq_lane = t0 + _lane_iota((n_sel, TQ))
    cur = tq_lane // SEL_BLOCK
    forced = (blk == 0) | (blk == cur) | (blk == cur - 1)
    future = blk > cur
    eye = jnp.where(_row_iota((TQ, TQ)) == _lane_iota((TQ, TQ)), 1.0, 0.0).astype(BF16)
    pad = jnp.zeros((LANES - n_sel, TQ), F32)
    q_wide = []
    for g in range(N_KV_GROUPS):
        psum = jnp.sum(p[g * HEADS_PER_GROUP:(g + 1) * HEADS_PER_GROUP], axis=0)
        ps_hi = psum.astype(BF16)
        ps_lo = (psum - ps_hi.astype(F32)).astype(BF16)
        imp = _dot_nt(ovt_ref[...], ps_hi) + _dot_nt(ovt_ref[...], ps_lo)
        val = jnp.where(forced, SEL_FORCE, jnp.where(future, NEG_INF, imp))
        rank = jnp.zeros((n_sel, TQ), jnp.int32)
        for d in range(1, n_sel):
            other = pltpu.roll(val, d, 0)
            ahead = (other > val) | ((other == val) & (blk >= d))
            rank = rank + ahead.astype(jnp.int32)
        dropped = jnp.concatenate([jnp.where(rank < N_SELECT, 0.0, 1.0), pad], axis=0)
        dropped = _dot_nt(eye, dropped.astype(BF16)).astype(BF16)
        r = slice(g * grp_rows, (g + 1) * grp_rows)
        q_wide.append(jnp.concatenate([qs[r], jnp.tile(dropped, (HEADS_PER_GROUP, 1))], axis=1))
    q_wide = jnp.concatenate(q_wide, axis=0)

    m_ref[...] = jnp.full(m_ref.shape, NEG_INF, F32)
    acc_ref[...] = jnp.zeros(acc_ref.shape, F32)

    def sel_tile(kt, causal):
        start = pl.multiple_of(kt * TK, TK)
        s = _dot_nt(q_wide, ks_ref[pl.ds(start, TK), :])
        if causal:
            bias = jnp.where(start + _lane_iota((TQ, TK)) <= t, 0.0, NEG_INF)
            s = (s.reshape(N_HEADS, TQ, TK) + bias[None]).reshape(rows, TK)
        m_old = m_ref[...]
        m_new = jnp.maximum(m_old, jnp.max(s, axis=-1, keepdims=True))
        p = jnp.exp2(s - jnp.tile(m_new, (1, TK // LANES))).astype(BF16)
        pv = _group_halves(_dot(p, vs_ref[pl.ds(start, TK), :]))
        acc_ref[...] = jnp.exp2(m_old - m_new) * acc_ref[...] + pv
        m_ref[...] = m_new

    last = (t0 + TQ - 1) // TK
    lax.fori_loop(0, last, lambda kt, carry: (sel_tile(kt, False), carry)[1], 0)
    sel_tile(last, True)
    acc = acc_ref[...]
    o_sel = acc / pltpu.roll(acc, HEAD_DIM, 1)

    heads = [a + b for a, b in zip(part, _gated(gates, 1, o_sel))]
    for j in range(HEADS_PER_GROUP):
        o_ref[:, j * LANES:(j + 1) * LANES] = jnp.where(
            lo, heads[j], heads[HEADS_PER_GROUP + j]).astype(BF16)


def _nsa(q, kc_raw, vc_raw, ks, vs, kw, vw, gates, pos_cmp, pek, pev, w1k, w1v, w2k, w2v,
         kg_row, inv_row, ovt):
    B, S, _ = q.shape
    assert S % TK == 0 and S >= WINDOW + TQ and LANES >= S // SEL_BLOCK
    n_rows = S // CMP_STRIDE
    kv_spec = lambda w: pl.BlockSpec((None, S, w), lambda b, i: (b, 0, 0))
    raw_spec = pl.BlockSpec((None, n_rows, CMP_STRIDE * KV_W), lambda b, i: (b, 0, 0))
    consts = (pek, pev, w1k, w1v, w2k, w2v, kg_row, inv_row, ovt)
    in_specs = [
        pl.BlockSpec((None, TQ, ATTN_W), lambda b, i: (b, i, 0)),
        raw_spec, raw_spec, kv_spec(2 * KV_W), kv_spec(2 * KV_W), kv_spec(KV_W), kv_spec(2 * KV_W),
        pl.BlockSpec((None, TQ, LANES), lambda b, i: (b, i, 0)),
        pl.BlockSpec((None, n_rows, 1), lambda b, i: (b, 0, 0)),
    ] + [_const_spec(c.shape) for c in consts]
    rows = N_HEADS * TQ
    return pl.pallas_call(
        _nsa_kernel,
        grid=(B, S // TQ),
        in_specs=in_specs,
        out_specs=pl.BlockSpec((None, TQ, ATTN_W), lambda b, i: (b, i, 0)),
        out_shape=jax.ShapeDtypeStruct((B, S, ATTN_W), BF16),
        scratch_shapes=[
            pltpu.VMEM((n_rows, KV_W), BF16),
            pltpu.VMEM((n_rows, 2 * KV_W), BF16),
            pltpu.VMEM((rows, LANES), F32),
            pltpu.VMEM((rows, LANES), F32),
        ],
        compiler_params=pltpu.CompilerParams(
            dimension_semantics=("arbitrary", "arbitrary"),
            vmem_limit_bytes=VMEM_LIMIT),
        name="nsa",
    )(q, kc_raw.reshape(B, n_rows, CMP_STRIDE * KV_W), vc_raw.reshape(B, n_rows, CMP_STRIDE * KV_W),
      ks, vs, kw, vw, gates, pos_cmp, *consts)


def _ffn_kernel(x_ref, zp_ref, gn_ref, a_ref, wno_ref, wo_ref, g_ref, wug_ref, wuv_ref,
                cw_ref, cb_ref, wd_ref, o_ref, carry_ref, act_ref):
    si = pl.program_id(1)
    tm = x_ref.shape[0]

    @pl.when(si == 0)
    def _():
        carry_ref[...] = jnp.zeros(carry_ref.shape, F32)

    y_nsa = _dot(a_ref[...], wno_ref[...])
    merged = zp_ref[...].astype(F32) + gn_ref[...].astype(F32) * y_nsa
    x1 = x_ref[...] + _dot(merged.astype(BF16), wo_ref[...])
    ms = jnp.mean(x1 * x1, axis=-1, keepdims=True)
    hb = (x1 * lax.rsqrt(ms + RMS_EPS) * g_ref[...]).astype(BF16)

    row = _row_iota((tm, FF_CHUNK))
    for c in range(D_FF // FF_CHUNK):
        sl = slice(c * FF_CHUNK, (c + 1) * FF_CHUNK)
        gp = _dot(hb, wug_ref[:, sl])
        val = _dot(hb, wuv_ref[:, sl])
        prev = carry_ref[:, sl]
        p1 = prev[SUBLANES - 1:SUBLANES, :]
        p2 = prev[SUBLANES - 2:SUBLANES - 1, :]
        lag1 = jnp.where(row == 0, p1, pltpu.roll(gp, 1, 0))
        lag2 = jnp.where(row == 0, p2, jnp.where(row == 1, p1, pltpu.roll(gp, 2, 0)))
        carry_ref[:, sl] = gp[tm - SUBLANES:, :]
        gc = (cw_ref[0:1, sl] * lag2 + cw_ref[1:2, sl] * lag1 + cw_ref[2:3, sl] * gp
              + cb_ref[:, sl])
        act_ref[:, sl] = (gc * jax.nn.sigmoid(gc) * val).astype(BF16)
    o_ref[...] = x1 + _dot(act_ref[...], wd_ref[...])


def _ffn(x, zp, gn, attn, wno, wo, g_row, wug, wuv, cw, cb_row, wd):
    B, S, _ = x.shape
    tm = TM_FFN
    tok = lambda w: pl.BlockSpec((None, tm, w), lambda b, s: (b, s, 0))
    consts = (wno, wo, g_row, wug, wuv, cw, cb_row, wd)
    return pl.pallas_call(
        _ffn_kernel,
        grid=(B, S // tm),
        in_specs=[tok(D_MODEL), tok(D_MODEL), tok(D_MODEL), tok(ATTN_W)]
                 + [_const_spec(c.shape) for c in consts],
        out_specs=tok(D_MODEL),
        out_shape=jax.ShapeDtypeStruct((B, S, D_MODEL), F32),
        scratch_shapes=[pltpu.VMEM((SUBLANES, D_FF), F32),
                        pltpu.VMEM((tm, D_FF), BF16)],
        compiler_params=pltpu.CompilerParams(
            dimension_semantics=("arbitrary", "arbitrary"),
            vmem_limit_bytes=VMEM_LIMIT),
        name="merge_ffn",
    )(x, zp, gn, attn, *consts)


def _slab_perm():
    j, half, d = np.meshgrid(np.arange(HEADS_PER_GROUP), np.arange(N_KV_GROUPS),
                             np.arange(HEAD_DIM), indexing="ij")
    return ((j + HEADS_PER_GROUP * half) * HEAD_DIM + d).reshape(-1)


def _group_block_diag(w):
    z = jnp.zeros_like(w)
    return jnp.concatenate([jnp.concatenate([w, z], axis=-1),
                            jnp.concatenate([z, w], axis=-1)], axis=-2)


def _compress_weights(pe, w1, w2):
    half = CMP_BLOCK // 2
    pe_rows = jnp.tile(pe.reshape(2, half, 1, HEAD_DIM), (1, 1, N_KV_GROUPS, 1)).reshape(2, -1)
    w1h = w1.reshape(2, half, HEAD_DIM, CMP_HIDDEN)
    w1bd = _group_block_diag(w1h).reshape(2, half * KV_W, N_KV_GROUPS * CMP_HIDDEN)
    return pe_rows.astype(F32), w1bd.astype(BF16), _group_block_diag(w2).astype(BF16)


def kernel(x, positions, mix_norm_g, w_in, q_norm_g, k_norm_g, cmp_pe, cmp_w1, cmp_w2, pool_w,
           pool_scale, w_pool_out, w_nsa_out, w_out, ffn_norm_g, w_up, conv_w, conv_b, w_down):
    B, S, _ = x.shape
    depth = w_in.shape[0]
    perm = _slab_perm()
    o_q = POOL_W
    o_kv = o_q + ATTN_W
    o_ng = o_kv + N_KV_SLOTS * KV_W
    o_mg = o_ng + N_HEADS * N_NSA_BRANCHES
    half = jnp.arange(ROT_HALF, dtype=F32)
    inv = ROPE_THETA ** (-(half * 2.0 / ROT_DIM))
    inv_row = jnp.tile(inv, LANES // ROT_HALF)[None, :]
    pos3 = positions[:, :, None]
    n_cmp_rows = S // CMP_STRIDE
    pos_cmp = positions[:, CMP_BLOCK - 1::CMP_STRIDE]
    pos_cmp = jnp.pad(pos_cmp, ((0, 0), (0, n_cmp_rows - pos_cmp.shape[1])))[:, :, None]
    n_sel = S // SEL_BLOCK
    n_cmp = (S - CMP_BLOCK) // CMP_STRIDE + 1
    sel_start = np.arange(n_sel)[:, None] * SEL_BLOCK
    cmp_start = np.arange(n_cmp)[None, :] * CMP_STRIDE
    ov = np.clip(np.minimum(sel_start + SEL_BLOCK, cmp_start + CMP_BLOCK)
                 - np.maximum(sel_start, cmp_start), 0, None) / CMP_BLOCK
    ovt = np.zeros((n_sel, n_cmp_rows), np.float32)
    ovt[:, :n_cmp] = ov
    ovt = jnp.asarray(ovt, BF16)

    for l in range(depth):
        w = w_in[l]
        wng = jnp.pad(w[:, o_ng:o_mg], ((0, 0), (0, LANES - (o_mg - o_ng))))
        q, kc_raw, vc_raw, ks, vs, kw, vw, gates, zp, gn = _in_proj(
            x, pos3, mix_norm_g[l][None, :],
            w[:, :o_q].astype(BF16), w[:, o_q:o_kv][:, perm].astype(BF16),
            w[:, o_kv:o_ng].astype(BF16), w[:, o_mg:].astype(BF16), wng.astype(BF16),
            inv_row, jnp.tile(q_norm_g[l], N_KV_GROUPS)[None, :],
            jnp.tile(k_norm_g[l], (1, N_KV_GROUPS)),
            pool_w[l].astype(BF16), pool_scale[l][None, :], w_pool_out[l].astype(BF16))
        pek, w1k, w2k = _compress_weights(cmp_pe[l, 0], cmp_w1[l, 0], cmp_w2[l, 0])
        pev, w1v, w2v = _compress_weights(cmp_pe[l, 1], cmp_w1[l, 1], cmp_w2[l, 1])
        attn = _nsa(q, kc_raw, vc_raw, ks, vs, kw, vw, gates, pos_cmp, pek, pev, w1k, w1v,
                    w2k, w2v, jnp.tile(k_norm_g[l, 0], N_KV_GROUPS)[None, :], inv_row, ovt)
        x = _ffn(x, zp, gn, attn, w_nsa_out[l][perm, :].astype(BF16), w_out[l].astype(BF16),
                 ffn_norm_g[l][None, :], w_up[l][:, :D_FF].astype(BF16),
                 w_up[l][:, D_FF:].astype(BF16), conv_w[l], conv_b[l][None, :],
                 w_down[l].astype(BF16))
    return x
```

```python
import functools

import numpy as np
import jax
import jax.numpy as jnp
from jax import lax
from jax.experimental import pallas as pl
from jax.experimental.pallas import tpu as pltpu

D_MODEL = 1024
POOL_WINDOWS = (2, 4, 8, 16)
POOL_GROUP_W = 128
POOL_W = len(POOL_WINDOWS) * POOL_GROUP_W
POOL_HALO = max(POOL_WINDOWS)
N_HEADS = 8
HEAD_DIM = 64
N_KV_GROUPS = 2
HEADS_PER_GROUP = N_HEADS // N_KV_GROUPS
ATTN_W = N_HEADS * HEAD_DIM
KV_W = N_KV_GROUPS * HEAD_DIM
N_KV_SLOTS = 6
N_NSA_BRANCHES = 3
CMP_BLOCK = 32
CMP_STRIDE = 16
CMP_HIDDEN = 128
SEL_BLOCK = 64
N_SELECT = 16
WINDOW = 512
ROPE_THETA = 500000.0
ROT_DIM = HEAD_DIM // 4
ROT_HALF = ROT_DIM // 2
D_FF = 2816
CONV_WIDTH = 3
RMS_EPS = 1e-6
NEG_INF = -1e30
SEL_FORCE = 1e9
LOG2_E = 1.4426950408889634

LANES = 128
SUBLANES = 8
VMEM_LIMIT = 56 * 1024 * 1024

TM_PROJ = 1024
TQ = 256
TK = 512
TM_FFN = 512
FF_CHUNK = 256

BF16 = jnp.bfloat16
F32 = jnp.float32


def _lane_iota(shape):
    return lax.broadcasted_iota(jnp.int32, shape, len(shape) - 1)


def _row_iota(shape):
    return lax.broadcasted_iota(jnp.int32, shape, len(shape) - 2)


def _dot(a, b):
    return jnp.dot(a, b, preferred_element_type=F32)


def _dot_nt(a, b):
    return lax.dot_general(a, b, (((1,), (1,)), ((), ())), preferred_element_type=F32)


def _half_norm(v, gain_row):
    lo = _lane_iota(v.shape) < HEAD_DIM
    sq = v * v
    ss_lo = jnp.sum(jnp.where(lo, sq, 0.0), axis=-1, keepdims=True)
    ss_hi = jnp.sum(jnp.where(lo, 0.0, sq), axis=-1, keepdims=True)
    ms = jnp.where(lo, ss_lo, ss_hi) * (1.0 / HEAD_DIM)
    return v * lax.rsqrt(ms + RMS_EPS) * gain_row


def _rope_factors(pos_col, inv_row):
    ang = pos_col.astype(F32) * inv_row
    d = _lane_iota(ang.shape) & (HEAD_DIM - 1)
    cos = jnp.where(d < ROT_DIM, jnp.cos(ang), 1.0)
    sin = jnp.sin(ang)
    sin = jnp.where(d < ROT_HALF, -sin, jnp.where(d < ROT_DIM, sin, 0.0))
    return cos, sin


def _rope(v, cos, sin):
    d = _lane_iota(v.shape) & (HEAD_DIM - 1)
    partner = jnp.where(d < ROT_HALF,
                        pltpu.roll(v, LANES - ROT_HALF, 1),
                        pltpu.roll(v, ROT_HALF, 1))
    return v * cos + partner * sin


def _in_proj_kernel(x_ref, pos_ref, g_ref, wu_ref, wq_ref, wkv_ref, wmg_ref, wng_ref,
                    inv_ref, qg_ref, kg_ref, pw_ref, ps_ref, wpo_ref,
                    q_out, kc_out, vc_out, ks_out, vs_out, kw_out, vw_out,
                    gate_out, zp_out, gn_out, ext_ref):
    si = pl.program_id(1)
    tm = x_ref.shape[0]

    @pl.when(si == 0)
    def _():
        ext_ref[0:POOL_HALO, :] = jnp.zeros((POOL_HALO, POOL_W), F32)

    x = x_ref[...]
    ms = jnp.mean(x * x, axis=-1, keepdims=True)
    hb = (x * lax.rsqrt(ms + RMS_EPS) * g_ref[...]).astype(BF16)

    cos, sin = _rope_factors(pos_ref[...], inv_ref[...])
    q = _dot(hb, wq_ref[...])
    scale = HEAD_DIM ** -0.5 * LOG2_E
    for j in range(ATTN_W // LANES):
        sl = slice(j * LANES, (j + 1) * LANES)
        qn = _rope(_half_norm(q[:, sl], qg_ref[...]), cos, sin)
        q_out[:, sl] = (qn * scale).astype(BF16)

    kv = _dot(hb, wkv_ref[...])
    kc_out[...] = kv[:, 0 * KV_W:1 * KV_W].astype(BF16)
    vc_out[...] = kv[:, 1 * KV_W:2 * KV_W].astype(BF16)
    ks = _half_norm(kv[:, 2 * KV_W:3 * KV_W], kg_ref[1:2, :])
    ks_out[:, :KV_W] = _rope(ks, cos, sin).astype(BF16)
    tok_blk = (si * tm + _row_iota((tm, LANES))) // SEL_BLOCK
    ks_out[:, KV_W:] = jnp.where(_lane_iota((tm, LANES)) == tok_blk, NEG_INF, 0.0).astype(BF16)
    v_sel = kv[:, 3 * KV_W:4 * KV_W]
    lo = _lane_iota(v_sel.shape) < HEAD_DIM
    vs_out[:, :KV_W] = jnp.where(lo, v_sel, 1.0).astype(BF16)
    vs_out[:, KV_W:] = jnp.where(lo, 1.0, v_sel).astype(BF16)
    kw = _half_norm(kv[:, 4 * KV_W:5 * KV_W], kg_ref[2:3, :])
    kw_out[...] = _rope(kw, cos, sin).astype(BF16)
    _store_value_ext(kv[:, 5 * KV_W:6 * KV_W], vw_out)

    u = _dot(hb, wu_ref[...])
    ext_ref[POOL_HALO:, :] = u
    t = si * tm + _row_iota((tm, 1))
    count = (t + 1).astype(F32)
    pooled = []
    for gi, w in enumerate(POOL_WINDOWS):
        sl = slice(gi * POOL_GROUP_W, (gi + 1) * POOL_GROUP_W)
        acc = ext_ref[:, sl]
        step = 1
        while step < w:
            acc = acc + pltpu.roll(acc, step, 0)
            step *= 2
        win_sum = acc[POOL_HALO:, :]
        mean = win_sum / jnp.minimum(count, float(w))
        pm = (mean - u[:, sl]).astype(BF16)
        pooled.append((_dot(pm, pw_ref[gi]) * ps_ref[:, sl]).astype(BF16))
    ext_ref[0:POOL_HALO, :] = ext_ref[tm:tm + POOL_HALO, :]
    y_pool = _dot(jnp.concatenate(pooled, axis=-1), wpo_ref[...])

    mg = _dot(hb, wmg_ref[...])
    zp_out[...] = (jax.nn.sigmoid(mg[:, :D_MODEL]) * y_pool).astype(BF16)
    gn_out[...] = jax.nn.sigmoid(mg[:, D_MODEL:]).astype(BF16)
    gate_out[...] = jax.nn.sigmoid(_dot(hb, wng_ref[...]))


def _store_value_ext(v, out):
    out[:, :KV_W] = v.astype(BF16)
    out[:, KV_W:] = jnp.ones(v.shape, BF16)


def _const_spec(shape):
    nd = len(shape)
    return pl.BlockSpec(shape, lambda b, s: (0,) * nd, pipeline_mode=pl.Buffered(1))


def _in_proj(x, pos3, g_row, wu, wq, wkv, wmg, wng, inv_row, qg_row, kg_rows, pw, ps_row, wpo):
    B, S, _ = x.shape
    tm = TM_PROJ
    tok = lambda w: pl.BlockSpec((None, tm, w), lambda b, s: (b, s, 0))
    out_shapes = (
        jax.ShapeDtypeStruct((B, S, ATTN_W), BF16),
        jax.ShapeDtypeStruct((B, S, KV_W), BF16),
        jax.ShapeDtypeStruct((B, S, KV_W), BF16),
        jax.ShapeDtypeStruct((B, S, 2 * KV_W), BF16),
        jax.ShapeDtypeStruct((B, S, 2 * KV_W), BF16),
        jax.ShapeDtypeStruct((B, S, KV_W), BF16),
        jax.ShapeDtypeStruct((B, S, 2 * KV_W), BF16),
        jax.ShapeDtypeStruct((B, S, LANES), F32),
        jax.ShapeDtypeStruct((B, S, D_MODEL), BF16),
        jax.ShapeDtypeStruct((B, S, D_MODEL), BF16),
    )
    out_specs = (tok(ATTN_W), tok(KV_W), tok(KV_W), tok(2 * KV_W), tok(2 * KV_W), tok(KV_W),
                 tok(2 * KV_W), tok(LANES), tok(D_MODEL), tok(D_MODEL))
    in_specs = [
        tok(D_MODEL),
        pl.BlockSpec((None, tm, 1), lambda b, s: (b, s, 0)),
        _const_spec(g_row.shape), _const_spec(wu.shape), _const_spec(wq.shape),
        _const_spec(wkv.shape), _const_spec(wmg.shape), _const_spec(wng.shape),
        _const_spec(inv_row.shape), _const_spec(qg_row.shape), _const_spec(kg_rows.shape),
        _const_spec(pw.shape), _const_spec(ps_row.shape), _const_spec(wpo.shape),
    ]
    return pl.pallas_call(
        _in_proj_kernel,
        grid=(B, S // tm),
        in_specs=in_specs,
        out_specs=out_specs,
        out_shape=out_shapes,
        scratch_shapes=[pltpu.VMEM((POOL_HALO + tm, POOL_W), F32)],
        compiler_params=pltpu.CompilerParams(
            dimension_semantics=("arbitrary", "arbitrary"),
            vmem_limit_bytes=VMEM_LIMIT),
        name="in_proj",
    )(x, pos3, g_row, wu, wq, wkv, wmg, wng, inv_row, qg_row, kg_rows, pw, ps_row, wpo)


def _compress(raw_ref, pe_ref, w1_ref, w2_ref):
    a = raw_ref[...].astype(F32)
    first = _dot((a + pe_ref[0:1, :]).astype(BF16), w1_ref[0])
    second = _dot((a + pe_ref[1:2, :]).astype(BF16), w1_ref[1])
    nblk = first.shape[0]
    hid = first + pltpu.roll(second, nblk - 1, 0)
    hid = hid * jax.nn.sigmoid(hid)
    return _dot(hid.astype(BF16), w2_ref[...])


def _gated(gates, branch, o):
    out = []
    for h in range(N_HEADS):
        c = N_NSA_BRANCHES * h + branch
        out.append(gates[:, c:c + 1] * o[h * TQ:(h + 1) * TQ])
    return out


def _group_halves(pv):
    grp_rows = pv.shape[0] // N_KV_GROUPS
    return jnp.concatenate([pv[g * grp_rows:(g + 1) * grp_rows, g * KV_W:(g + 1) * KV_W]
                            for g in range(N_KV_GROUPS)], axis=0)


def _nsa_kernel(q_ref, kcr_ref, vcr_ref, ks_ref, vs_ref, kw_ref, vw_ref, gate_ref, posc_ref,
                pek_ref, pev_ref, w1k_ref, w1v_ref, w2k_ref, w2v_ref, kg_ref, inv_ref, ovt_ref,
                o_ref, kc_s, vc_s, m_ref, acc_ref):
    qi = pl.program_id(1)
    seq = ks_ref.shape[0]
    grp_rows = HEADS_PER_GROUP * TQ

    @pl.when(qi == 0)
    def _():
        kc = _compress(kcr_ref, pek_ref, w1k_ref, w2k_ref)
        cos, sin = _rope_factors(posc_ref[...], inv_ref[...])
        kc_s[...] = _rope(_half_norm(kc, kg_ref[...]), cos, sin).astype(BF16)
        _store_value_ext(_compress(vcr_ref, pev_ref, w1v_ref, w2v_ref), vc_s)

    t0 = qi * TQ
    t = t0 + _row_iota((TQ, 1))
    lane = _lane_iota((TQ, LANES))
    lo = lane < HEAD_DIM

    q = q_ref[...]
    zero = jnp.zeros((TQ, LANES), BF16)
    slabs = [q[:, j * LANES:(j + 1) * LANES] for j in range(HEADS_PER_GROUP)]
    qs = jnp.concatenate([jnp.where(lo, s, zero) for s in slabs]
                         + [jnp.where(lo, zero, s) for s in slabs], axis=0)
    rows = N_HEADS * TQ

    s = _dot_nt(qs, kc_s[...])
    vis = lane * CMP_STRIDE + (CMP_BLOCK - 1) <= t
    s = (s.reshape(N_HEADS, TQ, LANES) + jnp.where(vis, 0.0, NEG_INF)[None]).reshape(rows, LANES)
    p = jnp.exp2(s - jnp.max(s, axis=-1, keepdims=True))
    pv = _dot(p.astype(BF16), vc_s[...])
    any_visible = t >= CMP_BLOCK - 1
    inv_l = 1.0 / pv[:, KV_W:]
    gates = gate_ref[...]
    part = [jnp.where(any_visible, o, 0.0) for o in _gated(gates, 0, pv[:, :KV_W] * inv_l)]
    p = (p * inv_l).reshape(N_HEADS, TQ, LANES)
    p = jnp.where(any_visible[None], p, 0.0)

    n_win = WINDOW + TQ
    start = pl.multiple_of(jnp.maximum(t0 - WINDOW, 0), TQ)
    dist = t - (start + _lane_iota((TQ, n_win)))
    bias = jnp.where((dist >= 0) & (dist < WINDOW), 0.0, NEG_INF)
    s = _dot_nt(qs, kw_ref[pl.ds(start, n_win), :])
    s = (s.reshape(N_HEADS, TQ, n_win) + bias[None]).reshape(rows, n_win)
    pw = jnp.exp2(s - jnp.max(s, axis=-1, keepdims=True)).astype(BF16)
    pv = _dot(pw, vw_ref[pl.ds(start, n_win), :])
    part = [a + b for a, b in zip(part, _gated(gates, 2, pv[:, :KV_W] / pv[:, KV_W:]))]

    n_sel = seq // SEL_BLOCK
    blk = _row_iota((n_sel, TQ))
    tq_lane = t0 + _lane_iota((n_sel, TQ))
    cur = tq_lane // SEL_BLOCK
    forced = (blk == 0) | (blk == cur) | (blk == cur - 1)
    future = blk > cur
    eye = jnp.where(_row_iota((TQ, TQ)) == _lane_iota((TQ, TQ)), 1.0, 0.0).astype(BF16)
    pad = jnp.zeros((LANES - n_sel, TQ), F32)
    q_wide = []
    for g in range(N_KV_GROUPS):
        psum = jnp.sum(p[g * HEADS_PER_GROUP:(g + 1) * HEADS_PER_GROUP], axis=0)
        ps_hi = psum.astype(BF16)
        ps_lo = (psum - ps_hi.astype(F32)).astype(BF16)
        imp = _dot_nt(ovt_ref[...], ps_hi) + _dot_nt(ovt_ref[...], ps_lo)
        val = jnp.where(forced, SEL_FORCE, jnp.where(future, NEG_INF, imp))
        rank = jnp.zeros((n_sel, TQ), jnp.int32)
        for d in range(1, n_sel):
            other = pltpu.roll(val, d, 0)
            ahead = (other > val) | ((other == val) & (blk >= d))
            rank = rank + ahead.astype(jnp.int32)
        dropped = jnp.concatenate([jnp.where(rank < N_SELECT, 0.0, 1.0), pad], axis=0)
        dropped = _dot_nt(eye, dropped.astype(BF16)).astype(BF16)
        r = slice(g * grp_rows, (g + 1) * grp_rows)
        q_wide.append(jnp.concatenate([qs[r], jnp.tile(dropped, (HEADS_PER_GROUP, 1))], axis=1))
    q_wide = jnp.concatenate(q_wide, axis=0)

    m_ref[...] = jnp.full(m_ref.shape, NEG_INF, F32)
    acc_ref[...] = jnp.zeros(acc_ref.shape, F32)

    def sel_tile(kt, causal):
        start = pl.multiple_of(kt * TK, TK)
        s = _dot_nt(q_wide, ks_ref[pl.ds(start, TK), :])
        if causal:
            bias = jnp.where(start + _lane_iota((TQ, TK)) <= t, 0.0, NEG_INF)
            s = (s.reshape(N_HEADS, TQ, TK) + bias[None]).reshape(rows, TK)
        m_old = m_ref[...]
        m_new = jnp.maximum(m_old, jnp.max(s, axis=-1, keepdims=True))
        p = jnp.exp2(s - jnp.tile(m_new, (1, TK // LANES))).astype(BF16)
        pv = _group_halves(_dot(p, vs_ref[pl.ds(start, TK), :]))
        acc_ref[...] = jnp.exp2(m_old - m_new) * acc_ref[...] + pv
        m_ref[...] = m_new

    last = (t0 + TQ - 1) // TK
    sel_tile(last, True)
    lax.fori_loop(0, last, lambda kt, carry: (sel_tile(kt, False), carry)[1], 0)
    acc = acc_ref[...]
    o_sel = acc / pltpu.roll(acc, HEAD_DIM, 1)

    heads = [a + b for a, b in zip(part, _gated(gates, 1, o_sel))]
    for j in range(HEADS_PER_GROUP):
        o_ref[:, j * LANES:(j + 1) * LANES] = jnp.where(
            lo, heads[j], heads[HEADS_PER_GROUP + j]).astype(BF16)


def _nsa(q, kc_raw, vc_raw, ks, vs, kw, vw, gates, pos_cmp, pek, pev, w1k, w1v, w2k, w2v,
         kg_row, inv_row, ovt):
    B, S, _ = q.shape
    assert S % TK == 0 and S >= WINDOW + TQ and LANES >= S // SEL_BLOCK
    n_rows = S // CMP_STRIDE
    kv_spec = lambda w: pl.BlockSpec((None, S, w), lambda b, i: (b, 0, 0))
    raw_spec = pl.BlockSpec((None, n_rows, CMP_STRIDE * KV_W), lambda b, i: (b, 0, 0))
    consts = (pek, pev, w1k, w1v, w2k, w2v, kg_row, inv_row, ovt)
    in_specs = [
        pl.BlockSpec((None, TQ, ATTN_W), lambda b, i: (b, i, 0)),
        raw_spec, raw_spec, kv_spec(2 * KV_W), kv_spec(2 * KV_W), kv_spec(KV_W), kv_spec(2 * KV_W),
        pl.BlockSpec((None, TQ, LANES), lambda b, i: (b, i, 0)),
        pl.BlockSpec((None, n_rows, 1), lambda b, i: (b, 0, 0)),
    ] + [_const_spec(c.shape) for c in consts]
    rows = N_HEADS * TQ
    return pl.pallas_call(
        _nsa_kernel,
        grid=(B, S // TQ),
        in_specs=in_specs,
        out_specs=pl.BlockSpec((None, TQ, ATTN_W), lambda b, i: (b, i, 0)),
        out_shape=jax.ShapeDtypeStruct((B, S, ATTN_W), BF16),
        scratch_shapes=[
            pltpu.VMEM((n_rows, KV_W), BF16),
            pltpu.VMEM((n_rows, 2 * KV_W), BF16),
            pltpu.VMEM((rows, LANES), F32),
            pltpu.VMEM((rows, LANES), F32),
        ],
        compiler_params=pltpu.CompilerParams(
            dimension_semantics=("arbitrary", "arbitrary"),
            vmem_limit_bytes=VMEM_LIMIT),
        name="nsa",
    )(q, kc_raw.reshape(B, n_rows, CMP_STRIDE * KV_W), vc_raw.reshape(B, n_rows, CMP_STRIDE * KV_W),
      ks, vs, kw, vw, gates, pos_cmp, *consts)


def _ffn_kernel(x_ref, zp_ref, gn_ref, a_ref, wno_ref, wo_ref, g_ref, wug_ref, wuv_ref,
                cw_ref, cb_ref, wd_ref, o_ref, carry_ref, act_ref):
    si = pl.program_id(1)
    tm = x_ref.shape[0]

    @pl.when(si == 0)
    def _():
        carry_ref[...] = jnp.zeros(carry_ref.shape, F32)

    y_nsa = _dot(a_ref[...], wno_ref[...])
    merged = zp_ref[...].astype(F32) + gn_ref[...].astype(F32) * y_nsa
    x1 = x_ref[...] + _dot(merged.astype(BF16), wo_ref[...])
    ms = jnp.mean(x1 * x1, axis=-1, keepdims=True)
    hb = (x1 * lax.rsqrt(ms + RMS_EPS) * g_ref[...]).astype(BF16)

    row = _row_iota((tm, FF_CHUNK))
    for c in range(D_FF // FF_CHUNK):
        sl = slice(c * FF_CHUNK, (c + 1) * FF_CHUNK)
        gp = _dot(hb, wug_ref[:, sl])
        val = _dot(hb, wuv_ref[:, sl])
        prev = carry_ref[:, sl]
        p1 = prev[SUBLANES - 1:SUBLANES, :]
        p2 = prev[SUBLANES - 2:SUBLANES - 1, :]
        lag1 = jnp.where(row == 0, p1, pltpu.roll(gp, 1, 0))
        lag2 = jnp.where(row == 0, p2, jnp.where(row == 1, p1, pltpu.roll(gp, 2, 0)))
        carry_ref[:, sl] = gp[tm - SUBLANES:, :]
        gc = (cw_ref[0:1, sl] * lag2 + cw_ref[1:2, sl] * lag1 + cw_ref[2:3, sl] * gp
              + cb_ref[:, sl])
        act_ref[:, sl] = (gc * jax.nn.sigmoid(gc) * val).astype(BF16)
    o_ref[...] = x1 + _dot(act_ref[...], wd_ref[...])


def _ffn(x, zp, gn, attn, wno, wo, g_row, wug, wuv, cw, cb_row, wd):
    B, S, _ = x.shape
    tm = TM_FFN
    tok = lambda w: pl.BlockSpec((None, tm, w), lambda b, s: (b, s, 0))
    consts = (wno, wo, g_row, wug, wuv, cw, cb_row, wd)
    return pl.pallas_call(
        _ffn_kernel,
        grid=(B, S // tm),
        in_specs=[tok(D_MODEL), tok(D_MODEL), tok(D_MODEL), tok(ATTN_W)]
                 + [_const_spec(c.shape) for c in consts],
        out_specs=tok(D_MODEL),
        out_shape=jax.ShapeDtypeStruct((B, S, D_MODEL), F32),
        scratch_shapes=[pltpu.VMEM((SUBLANES, D_FF), F32),
                        pltpu.VMEM((tm, D_FF), BF16)],
        compiler_params=pltpu.CompilerParams(
            dimension_semantics=("arbitrary", "arbitrary"),
            vmem_limit_bytes=VMEM_LIMIT),
        name="merge_ffn",
    )(x, zp, gn, attn, *consts)


def _slab_perm():
    j, half, d = np.meshgrid(np.arange(HEADS_PER_GROUP), np.arange(N_KV_GROUPS),
                             np.arange(HEAD_DIM), indexing="ij")
    return ((j + HEADS_PER_GROUP * half) * HEAD_DIM + d).reshape(-1)


def _group_block_diag(w):
    z = jnp.zeros_like(w)
    return jnp.concatenate([jnp.concatenate([w, z], axis=-1),
                            jnp.concatenate([z, w], axis=-1)], axis=-2)


def _compress_weights(pe, w1, w2):
    half = CMP_BLOCK // 2
    pe_rows = jnp.tile(pe.reshape(2, half, 1, HEAD_DIM), (1, 1, N_KV_GROUPS, 1)).reshape(2, -1)
    w1h = w1.reshape(2, half, HEAD_DIM, CMP_HIDDEN)
    w1bd = _group_block_diag(w1h).reshape(2, half * KV_W, N_KV_GROUPS * CMP_HIDDEN)
    return pe_rows.astype(F32), w1bd.astype(BF16), _group_block_diag(w2).astype(BF16)


def kernel(x, positions, mix_norm_g, w_in, q_norm_g, k_norm_g, cmp_pe, cmp_w1, cmp_w2, pool_w,
           pool_scale, w_pool_out, w_nsa_out, w_out, ffn_norm_g, w_up, conv_w, conv_b, w_down):
    B, S, _ = x.shape
    depth = w_in.shape[0]
    perm = _slab_perm()
    o_q = POOL_W
    o_kv = o_q + ATTN_W
    o_ng = o_kv + N_KV_SLOTS * KV_W
    o_mg = o_ng + N_HEADS * N_NSA_BRANCHES
    half = jnp.arange(ROT_HALF, dtype=F32)
    inv = ROPE_THETA ** (-(half * 2.0 / ROT_DIM))
    inv_row = jnp.tile(inv, LANES // ROT_HALF)[None, :]
    pos3 = positions[:, :, None]
    n_cmp_rows = S // CMP_STRIDE
    pos_cmp = positions[:, CMP_BLOCK - 1::CMP_STRIDE]
    pos_cmp = jnp.pad(pos_cmp, ((0, 0), (0, n_cmp_rows - pos_cmp.shape[1])))[:, :, None]
    n_sel = S // SEL_BLOCK
    n_cmp = (S - CMP_BLOCK) // CMP_STRIDE + 1
    sel_start = np.arange(n_sel)[:, None] * SEL_BLOCK
    cmp_start = np.arange(n_cmp)[None, :] * CMP_STRIDE
    ov = np.clip(np.minimum(sel_start + SEL_BLOCK, cmp_start + CMP_BLOCK)
                 - np.maximum(sel_start, cmp_start), 0, None) / CMP_BLOCK
    ovt = np.zeros((n_sel, n_cmp_rows), np.float32)
    ovt[:, :n_cmp] = ov
    ovt = jnp.asarray(ovt, BF16)

    for l in range(depth):
        w = w_in[l]
        wng = jnp.pad(w[:, o_ng:o_mg], ((0, 0), (0, LANES - (o_mg - o_ng))))
        q, kc_raw, vc_raw, ks, vs, kw, vw, gates, zp, gn = _in_proj(
            x, pos3, mix_norm_g[l][None, :],
            w[:, :o_q].astype(BF16), w[:, o_q:o_kv][:, perm].astype(BF16),
            w[:, o_kv:o_ng].astype(BF16), w[:, o_mg:].astype(BF16), wng.astype(BF16),
            inv_row, jnp.tile(q_norm_g[l], N_KV_GROUPS)[None, :],
            jnp.tile(k_norm_g[l], (1, N_KV_GROUPS)),
            pool_w[l].astype(BF16), pool_scale[l][None, :], w_pool_out[l].astype(BF16))
        pek, w1k, w2k = _compress_weights(cmp_pe[l, 0], cmp_w1[l, 0], cmp_w2[l, 0])
        pev, w1v, w2v = _compress_weights(cmp_pe[l, 1], cmp_w1[l, 1], cmp_w2[l, 1])
        attn = _nsa(q, kc_raw, vc_raw, ks, vs, kw, vw, gates, pos_cmp, pek, pev, w1k, w1v,
                    w2k, w2v, jnp.tile(k_norm_g[l, 0], N_KV_GROUPS)[None, :], inv_row, ovt)
        x = _ffn(x, zp, gn, attn, w_nsa_out[l][perm, :].astype(BF16), w_out[l].astype(BF16),
                 ffn_norm_g[l][None, :], w_up[l][:, :D_FF].astype(BF16),
                 w_up[l][:, D_FF:].astype(BF16), conv_w[l], conv_b[l][None, :],
                 w_down[l].astype(BF16))
    return x
```

```python
import functools

import numpy as np
import jax
import jax.numpy as jnp
from jax import lax
from jax.experimental import pallas as pl
from jax.experimental.pallas import tpu as pltpu

D_MODEL = 1024
POOL_WINDOWS = (2, 4, 8, 16)
POOL_GROUP_W = 128
POOL_W = len(POOL_WINDOWS) * POOL_GROUP_W
POOL_HALO = max(POOL_WINDOWS)
N_HEADS = 8
HEAD_DIM = 64
N_KV_GROUPS = 2
HEADS_PER_GROUP = N_HEADS // N_KV_GROUPS
ATTN_W = N_HEADS * HEAD_DIM
KV_W = N_KV_GROUPS * HEAD_DIM
N_KV_SLOTS = 6
N_NSA_BRANCHES = 3
CMP_BLOCK = 32
CMP_STRIDE = 16
CMP_HIDDEN = 128
SEL_BLOCK = 64
N_SELECT = 16
WINDOW = 512
ROPE_THETA = 500000.0
ROT_DIM = HEAD_DIM // 4
ROT_HALF = ROT_DIM // 2
D_FF = 2816
CONV_WIDTH = 3
RMS_EPS = 1e-6
NEG_INF = -1e30
SEL_FORCE = 1e9
LOG2_E = 1.4426950408889634

LANES = 128
SUBLANES = 8
VMEM_LIMIT = 56 * 1024 * 1024

TM_PROJ = 1024
TQ = 256
TK = 512
TM_FFN = 512
FF_CHUNK = 256

BF16 = jnp.bfloat16
F32 = jnp.float32


def _lane_iota(shape):
    return lax.broadcasted_iota(jnp.int32, shape, len(shape) - 1)


def _row_iota(shape):
    return lax.broadcasted_iota(jnp.int32, shape, len(shape) - 2)


def _dot(a, b):
    return jnp.dot(a, b, preferred_element_type=F32)


def _dot_nt(a, b):
    return lax.dot_general(a, b, (((1,), (1,)), ((), ())), preferred_element_type=F32)


def _half_norm(v, gain_row):
    lo = _lane_iota(v.shape) < HEAD_DIM
    sq = v * v
    ss_lo = jnp.sum(jnp.where(lo, sq, 0.0), axis=-1, keepdims=True)
    ss_hi = jnp.sum(jnp.where(lo, 0.0, sq), axis=-1, keepdims=True)
    ms = jnp.where(lo, ss_lo, ss_hi) * (1.0 / HEAD_DIM)
    return v * lax.rsqrt(ms + RMS_EPS) * gain_row


def _rope_factors(pos_col, inv_row):
    ang = pos_col.astype(F32) * inv_row
    d = _lane_iota(ang.shape) & (HEAD_DIM - 1)
    cos = jnp.where(d < ROT_DIM, jnp.cos(ang), 1.0)
    sin = jnp.sin(ang)
    sin = jnp.where(d < ROT_HALF, -sin, jnp.where(d < ROT_DIM, sin, 0.0))
    return cos, sin


def _rope(v, cos, sin):
    d = _lane_iota(v.shape) & (HEAD_DIM - 1)
    partner = jnp.where(d < ROT_HALF,
                        pltpu.roll(v, LANES - ROT_HALF, 1),
                        pltpu.roll(v, ROT_HALF, 1))
    return v * cos + partner * sin


def _in_proj_kernel(x_ref, pos_ref, g_ref, wu_ref, wq_ref, wkv_ref, wmg_ref, wng_ref,
                    inv_ref, qg_ref, kg_ref, pw_ref, ps_ref, wpo_ref,
                    q_out, kc_out, vc_out, ks_out, vs_out, kw_out, vw_out,
                    gate_out, zp_out, gn_out, ext_ref):
    si = pl.program_id(1)
    tm = x_ref.shape[0]

    @pl.when(si == 0)
    def _():
        ext_ref[0:POOL_HALO, :] = jnp.zeros((POOL_HALO, POOL_W), F32)

    x = x_ref[...]
    ms = jnp.mean(x * x, axis=-1, keepdims=True)
    hb = (x * lax.rsqrt(ms + RMS_EPS) * g_ref[...]).astype(BF16)

    cos, sin = _rope_factors(pos_ref[...], inv_ref[...])
    q = _dot(hb, wq_ref[...])
    scale = HEAD_DIM ** -0.5 * LOG2_E
    for j in range(ATTN_W // LANES):
        sl = slice(j * LANES, (j + 1) * LANES)
        qn = _rope(_half_norm(q[:, sl], qg_ref[...]), cos, sin)
        q_out[:, sl] = (qn * scale).astype(BF16)

    kv = _dot(hb, wkv_ref[...])
    kc_out[...] = kv[:, 0 * KV_W:1 * KV_W].astype(BF16)
    vc_out[...] = kv[:, 1 * KV_W:2 * KV_W].astype(BF16)
    ks = _half_norm(kv[:, 2 * KV_W:3 * KV_W], kg_ref[1:2, :])
    ks_out[:, :KV_W] = _rope(ks, cos, sin).astype(BF16)
    tok_blk = (si * tm + _row_iota((tm, LANES))) // SEL_BLOCK
    ks_out[:, KV_W:] = jnp.where(_lane_iota((tm, LANES)) == tok_blk, NEG_INF, 0.0).astype(BF16)
    v_sel = kv[:, 3 * KV_W:4 * KV_W]
    lo = _lane_iota(v_sel.shape) < HEAD_DIM
    vs_out[:, :KV_W] = jnp.where(lo, v_sel, 1.0).astype(BF16)
    vs_out[:, KV_W:] = jnp.where(lo, 1.0, v_sel).astype(BF16)
    kw = _half_norm(kv[:, 4 * KV_W:5 * KV_W], kg_ref[2:3, :])
    kw_out[...] = _rope(kw, cos, sin).astype(BF16)
    _store_value_ext(kv[:, 5 * KV_W:6 * KV_W], vw_out)

    u = _dot(hb, wu_ref[...])
    ext_ref[POOL_HALO:, :] = u
    t = si * tm + _row_iota((tm, 1))
    count = (t + 1).astype(F32)
    pooled = []
    for gi, w in enumerate(POOL_WINDOWS):
        sl = slice(gi * POOL_GROUP_W, (gi + 1) * POOL_GROUP_W)
        acc = ext_ref[:, sl]
        step = 1
        while step < w:
            acc = acc + pltpu.roll(acc, step, 0)
            step *= 2
        win_sum = acc[POOL_HALO:, :]
        mean = win_sum / jnp.minimum(count, float(w))
        pm = (mean - u[:, sl]).astype(BF16)
        pooled.append((_dot(pm, pw_ref[gi]) * ps_ref[:, sl]).astype(BF16))
    ext_ref[0:POOL_HALO, :] = ext_ref[tm:tm + POOL_HALO, :]
    y_pool = _dot(jnp.concatenate(pooled, axis=-1), wpo_ref[...])

    mg = _dot(hb, wmg_ref[...])
    zp_out[...] = (jax.nn.sigmoid(mg[:, :D_MODEL]) * y_pool).astype(BF16)
    gn_out[...] = jax.nn.sigmoid(mg[:, D_MODEL:]).astype(BF16)
    gate_out[...] = jax.nn.sigmoid(_dot(hb, wng_ref[...]))


def _store_value_ext(v, out):
    out[:, :KV_W] = v.astype(BF16)
    out[:, KV_W:] = jnp.ones(v.shape, BF16)


def _const_spec(shape):
    nd = len(shape)
    return pl.BlockSpec(shape, lambda b, s: (0,) * nd, pipeline_mode=pl.Buffered(1))


def _in_proj(x, pos3, g_row, wu, wq, wkv, wmg, wng, inv_row, qg_row, kg_rows, pw, ps_row, wpo):
    B, S, _ = x.shape
    tm = TM_PROJ
    tok = lambda w: pl.BlockSpec((None, tm, w), lambda b, s: (b, s, 0))
    out_shapes = (
        jax.ShapeDtypeStruct((B, S, ATTN_W), BF16),
        jax.ShapeDtypeStruct((B, S, KV_W), BF16),
        jax.ShapeDtypeStruct((B, S, KV_W), BF16),
        jax.ShapeDtypeStruct((B, S, 2 * KV_W), BF16),
        jax.ShapeDtypeStruct((B, S, 2 * KV_W), BF16),
        jax.ShapeDtypeStruct((B, S, KV_W), BF16),
        jax.ShapeDtypeStruct((B, S, 2 * KV_W), BF16),
        jax.ShapeDtypeStruct((B, S, LANES), F32),
        jax.ShapeDtypeStruct((B, S, D_MODEL), BF16),
        jax.ShapeDtypeStruct((B, S, D_MODEL), BF16),
    )
    out_specs = (tok(ATTN_W), tok(KV_W), tok(KV_W), tok(2 * KV_W), tok(2 * KV_W), tok(KV_W),
                 tok(2 * KV_W), tok(LANES), tok(D_MODEL), tok(D_MODEL))
    in_specs = [
        tok(D_MODEL),
        pl.BlockSpec((None, tm, 1), lambda b, s: (b, s, 0)),
        _const_spec(g_row.shape), _const_spec(wu.shape), _const_spec(wq.shape),
        _const_spec(wkv.shape), _const_spec(wmg.shape), _const_spec(wng.shape),
        _const_spec(inv_row.shape), _const_spec(qg_row.shape), _const_spec(kg_rows.shape),
        _const_spec(pw.shape), _const_spec(ps_row.shape), _const_spec(wpo.shape),
    ]
    return pl.pallas_call(
        _in_proj_kernel,
        grid=(B, S // tm),
        in_specs=in_specs,
        out_specs=out_specs,
        out_shape=out_shapes,
        scratch_shapes=[pltpu.VMEM((POOL_HALO + tm, POOL_W), F32)],
        compiler_params=pltpu.CompilerParams(
            dimension_semantics=("arbitrary", "arbitrary"),
            vmem_limit_bytes=VMEM_LIMIT),
        name="in_proj",
    )(x, pos3, g_row, wu, wq, wkv, wmg, wng, inv_row, qg_row, kg_rows, pw, ps_row, wpo)


def _compress(raw_ref, pe_ref, w1_ref, w2_ref):
    a = raw_ref[...].astype(F32)
    first = _dot((a + pe_ref[0:1, :]).astype(BF16), w1_ref[0])
    second = _dot((a + pe_ref[1:2, :]).astype(BF16), w1_ref[1])
    nblk = first.shape[0]
    hid = first + pltpu.roll(second, nblk - 1, 0)
    hid = hid * jax.nn.sigmoid(hid)
    return _dot(hid.astype(BF16), w2_ref[...])


def _gated(gates, branch, o):
    out = []
    for h in range(N_HEADS):
        c = N_NSA_BRANCHES * h + branch
        out.append(gates[:, c:c + 1] * o[h * TQ:(h + 1) * TQ])
    return out


def _group_halves(pv):
    grp_rows = pv.shape[0] // N_KV_GROUPS
    return jnp.concatenate([pv[g * grp_rows:(g + 1) * grp_rows, g * KV_W:(g + 1) * KV_W]
                            for g in range(N_KV_GROUPS)], axis=0)


def _nsa_kernel(q_ref, kcr_ref, vcr_ref, ks_ref, vs_ref, kw_ref, vw_ref, gate_ref, posc_ref,
                pek_ref, pev_ref, w1k_ref, w1v_ref, w2k_ref, w2v_ref, kg_ref, inv_ref, ovt_ref,
                o_ref, kc_s, vc_s, m_ref, acc_ref):
    qi = pl.program_id(1)

    @pl.when(qi == 0)
    def _():
        kc = _compress(kcr_ref, pek_ref, w1k_ref, w2k_ref)
        cos, sin = _rope_factors(posc_ref[...], inv_ref[...])
        kc_s[...] = _rope(_half_norm(kc, kg_ref[...]), cos, sin).astype(BF16)
        _store_value_ext(_compress(vcr_ref, pev_ref, w1v_ref, w2v_ref), vc_s)

    for last in range(ks_ref.shape[0] // TK):
        pl.when((qi * TQ + TQ - 1) // TK == last)(functools.partial(
            _nsa_tile, last, qi * TQ, q_ref, ks_ref, vs_ref, kw_ref, vw_ref, gate_ref, ovt_ref,
            o_ref, kc_s, vc_s, m_ref, acc_ref))


def _nsa_tile(last, t0, q_ref, ks_ref, vs_ref, kw_ref, vw_ref, gate_ref, ovt_ref, o_ref,
              kc_s, vc_s, m_ref, acc_ref):
    seq = ks_ref.shape[0]
    grp_rows = HEADS_PER_GROUP * TQ
    t = t0 + _row_iota((TQ, 1))
    lane = _lane_iota((TQ, LANES))
    lo = lane < HEAD_DIM

    q = q_ref[...]
    zero = jnp.zeros((TQ, LANES), BF16)
    slabs = [q[:, j * LANES:(j + 1) * LANES] for j in range(HEADS_PER_GROUP)]
    qs = jnp.concatenate([jnp.where(lo, s, zero) for s in slabs]
                         + [jnp.where(lo, zero, s) for s in slabs], axis=0)
    rows = N_HEADS * TQ

    s = _dot_nt(qs, kc_s[...])
    vis = lane * CMP_STRIDE + (CMP_BLOCK - 1) <= t
    s = (s.reshape(N_HEADS, TQ, LANES) + jnp.where(vis, 0.0, NEG_INF)[None]).reshape(rows, LANES)
    p = jnp.exp2(s - jnp.max(s, axis=-1, keepdims=True))
    pv = _dot(p.astype(BF16), vc_s[...])
    any_visible = t >= CMP_BLOCK - 1
    inv_l = 1.0 / pv[:, KV_W:]
    gates = gate_ref[...]
    part = [jnp.where(any_visible, o, 0.0) for o in _gated(gates, 0, pv[:, :KV_W] * inv_l)]
    p = (p * inv_l).reshape(N_HEADS, TQ, LANES)
    p = jnp.where(any_visible[None], p, 0.0)

    n_win = WINDOW + TQ
    start = pl.multiple_of(jnp.maximum(t0 - WINDOW, 0), TQ)
    dist = t - (start + _lane_iota((TQ, n_win)))
    bias = jnp.where((dist >= 0) & (dist < WINDOW), 0.0, NEG_INF)
    s = _dot_nt(qs, kw_ref[pl.ds(start, n_win), :])
    s = (s.reshape(N_HEADS, TQ, n_win) + bias[None]).reshape(rows, n_win)
    pw = jnp.exp2(s - jnp.max(s, axis=-1, keepdims=True)).astype(BF16)
    pv = _dot(pw, vw_ref[pl.ds(start, n_win), :])
    part = [a + b for a, b in zip(part, _gated(gates, 2, pv[:, :KV_W] / pv[:, KV_W:]))]

    n_sel = seq // SEL_BLOCK
    blk = _row_iota((n_sel, TQ))
    tq_lane = t0 + _lane_iota((n_sel, TQ))
    cur = tq_lane // SEL_BLOCK
    forced = (blk == 0) | (blk == cur) | (blk == cur - 1)
    future = blk > cur
    eye = jnp.where(_row_iota((TQ, TQ)) == _lane_iota((TQ, TQ)), 1.0, 0.0).astype(BF16)
    pad = jnp.zeros((LANES - n_sel, TQ), F32)
    q_wide = []
    for g in range(N_KV_GROUPS):
        psum = jnp.sum(p[g * HEADS_PER_GROUP:(g + 1) * HEADS_PER_GROUP], axis=0)
        ps_hi = psum.astype(BF16)
        ps_lo = (psum - ps_hi.astype(F32)).astype(BF16)
        imp = _dot_nt(ovt_ref[...], ps_hi) + _dot_nt(ovt_ref[...], ps_lo)
        val = jnp.where(forced, SEL_FORCE, jnp.where(future, NEG_INF, imp))
        rank = jnp.zeros((n_sel, TQ), jnp.int32)
        for d in range(1, n_sel):
            other = pltpu.roll(val, d, 0)
            ahead = (other > val) | ((other == val) & (blk >= d))
            rank = rank + ahead.astype(jnp.int32)
        dropped = jnp.concatenate([jnp.where(rank < N_SELECT, 0.0, 1.0), pad], axis=0)
        dropped = _dot_nt(eye, dropped.astype(BF16)).astype(BF16)
        r = slice(g * grp_rows, (g + 1) * grp_rows)
        q_wide.append(jnp.concatenate([qs[r], jnp.tile(dropped, (HEADS_PER_GROUP, 1))], axis=1))
    q_wide = jnp.concatenate(q_wide, axis=0)

    m_ref[...] = jnp.full(m_ref.shape, NEG_INF, F32)
    acc_ref[...] = jnp.zeros(acc_ref.shape, F32)

    def sel_tile(kt, causal):
        start = kt * TK
        s = _dot_nt(q_wide, ks_ref[pl.ds(start, TK), :])
        if causal:
            bias = jnp.where(start + _lane_iota((TQ, TK)) <= t, 0.0, NEG_INF)
            s = (s.reshape(N_HEADS, TQ, TK) + bias[None]).reshape(rows, TK)
        m_old = m_ref[...]
        m_new = jnp.maximum(m_old, jnp.max(s, axis=-1, keepdims=True))
        p = jnp.exp2(s - jnp.tile(m_new, (1, TK // LANES))).astype(BF16)
        pv = _group_halves(_dot(p, vs_ref[pl.ds(start, TK), :]))
        acc_ref[...] = jnp.exp2(m_old - m_new) * acc_ref[...] + pv
        m_ref[...] = m_new

    sel_tile(last, True)
    for kt in range(last):
        sel_tile(kt, False)
    acc = acc_ref[...]
    o_sel = acc / pltpu.roll(acc, HEAD_DIM, 1)

    heads = [a + b for a, b in zip(part, _gated(gates, 1, o_sel))]
    for j in range(HEADS_PER_GROUP):
        o_ref[:, j * LANES:(j + 1) * LANES] = jnp.where(
            lo, heads[j], heads[HEADS_PER_GROUP + j]).astype(BF16)


def _nsa(q, kc_raw, vc_raw, ks, vs, kw, vw, gates, pos_cmp, pek, pev, w1k, w1v, w2k, w2v,
         kg_row, inv_row, ovt):
    B, S, _ = q.shape
    assert S % TK == 0 and S >= WINDOW + TQ and LANES >= S // SEL_BLOCK
    n_rows = S // CMP_STRIDE
    kv_spec = lambda w: pl.BlockSpec((None, S, w), lambda b, i: (b, 0, 0))
    raw_spec = pl.BlockSpec((None, n_rows, CMP_STRIDE * KV_W), lambda b, i: (b, 0, 0))
    consts = (pek, pev, w1k, w1v, w2k, w2v, kg_row, inv_row, ovt)
    in_specs = [
        pl.BlockSpec((None, TQ, ATTN_W), lambda b, i: (b, i, 0)),
        raw_spec, raw_spec, kv_spec(2 * KV_W), kv_spec(2 * KV_W), kv_spec(KV_W), kv_spec(2 * KV_W),
        pl.BlockSpec((None, TQ, LANES), lambda b, i: (b, i, 0)),
        pl.BlockSpec((None, n_rows, 1), lambda b, i: (b, 0, 0)),
    ] + [_const_spec(c.shape) for c in consts]
    rows = N_HEADS * TQ
    return pl.pallas_call(
        _nsa_kernel,
        grid=(B, S // TQ),
        in_specs=in_specs,
        out_specs=pl.BlockSpec((None, TQ, ATTN_W), lambda b, i: (b, i, 0)),
        out_shape=jax.ShapeDtypeStruct((B, S, ATTN_W), BF16),
        scratch_shapes=[
            pltpu.VMEM((n_rows, KV_W), BF16),
            pltpu.VMEM((n_rows, 2 * KV_W), BF16),
            pltpu.VMEM((rows, LANES), F32),
            pltpu.VMEM((rows, LANES), F32),
        ],
        compiler_params=pltpu.CompilerParams(
            dimension_semantics=("arbitrary", "arbitrary"),
            vmem_limit_bytes=VMEM_LIMIT),
        name="nsa",
    )(q, kc_raw.reshape(B, n_rows, CMP_STRIDE * KV_W), vc_raw.reshape(B, n_rows, CMP_STRIDE * KV_W),
      ks, vs, kw, vw, gates, pos_cmp, *consts)


def _ffn_kernel(x_ref, zp_ref, gn_ref, a_ref, wno_ref, wo_ref, g_ref, wug_ref, wuv_ref,
                cw_ref, cb_ref, wd_ref, o_ref, carry_ref, act_ref):
    si = pl.program_id(1)
    tm = x_ref.shape[0]

    @pl.when(si == 0)
    def _():
        carry_ref[...] = jnp.zeros(carry_ref.shape, F32)

    y_nsa = _dot(a_ref[...], wno_ref[...])
    merged = zp_ref[...].astype(F32) + gn_ref[...].astype(F32) * y_nsa
    x1 = x_ref[...] + _dot(merged.astype(BF16), wo_ref[...])
    ms = jnp.mean(x1 * x1, axis=-1, keepdims=True)
    hb = (x1 * lax.rsqrt(ms + RMS_EPS) * g_ref[...]).astype(BF16)

    row = _row_iota((tm, FF_CHUNK))
    for c in range(D_FF // FF_CHUNK):
        sl = slice(c * FF_CHUNK, (c + 1) * FF_CHUNK)
        gp = _dot(hb, wug_ref[:, sl])
        val = _dot(hb, wuv_ref[:, sl])
        prev = carry_ref[:, sl]
        p1 = prev[SUBLANES - 1:SUBLANES, :]
        p2 = prev[SUBLANES - 2:SUBLANES - 1, :]
        lag1 = jnp.where(row == 0, p1, pltpu.roll(gp, 1, 0))
        lag2 = jnp.where(row == 0, p2, jnp.where(row == 1, p1, pltpu.roll(gp, 2, 0)))
        carry_ref[:, sl] = gp[tm - SUBLANES:, :]
        gc = (cw_ref[0:1, sl] * lag2 + cw_ref[1:2, sl] * lag1 + cw_ref[2:3, sl] * gp
              + cb_ref[:, sl])
        act_ref[:, sl] = (gc * jax.nn.sigmoid(gc) * val).astype(BF16)
    o_ref[...] = x1 + _dot(act_ref[...], wd_ref[...])


def _ffn(x, zp, gn, attn, wno, wo, g_row, wug, wuv, cw, cb_row, wd):
    B, S, _ = x.shape
    tm = TM_FFN
    tok = lambda w: pl.BlockSpec((None, tm, w), lambda b, s: (b, s, 0))
    consts = (wno, wo, g_row, wug, wuv, cw, cb_row, wd)
    return pl.pallas_call(
        _ffn_kernel,
        grid=(B, S // tm),
        in_specs=[tok(D_MODEL), tok(D_MODEL), tok(D_MODEL), tok(ATTN_W)]
                 + [_const_spec(c.shape) for c in consts],
        out_specs=tok(D_MODEL),
        out_shape=jax.ShapeDtypeStruct((B, S, D_MODEL), F32),
        scratch_shapes=[pltpu.VMEM((SUBLANES, D_FF), F32),
                        pltpu.VMEM((tm, D_FF), BF16)],
        compiler_params=pltpu.CompilerParams(
            dimension_semantics=("arbitrary", "arbitrary"),
            vmem_limit_bytes=VMEM_LIMIT),
        name="merge_ffn",
    )(x, zp, gn, attn, *consts)


def _slab_perm():
    j, half, d = np.meshgrid(np.arange(HEADS_PER_GROUP), np.arange(N_KV_GROUPS),
                             np.arange(HEAD_DIM), indexing="ij")
    return ((j + HEADS_PER_GROUP * half) * HEAD_DIM + d).reshape(-1)


def _group_block_diag(w):
    z = jnp.zeros_like(w)
    return jnp.concatenate([jnp.concatenate([w, z], axis=-1),
                            jnp.concatenate([z, w], axis=-1)], axis=-2)


def _compress_weights(pe, w1, w2):
    half = CMP_BLOCK // 2
    pe_rows = jnp.tile(pe.reshape(2, half, 1, HEAD_DIM), (1, 1, N_KV_GROUPS, 1)).reshape(2, -1)
    w1h = w1.reshape(2, half, HEAD_DIM, CMP_HIDDEN)
    w1bd = _group_block_diag(w1h).reshape(2, half * KV_W, N_KV_GROUPS * CMP_HIDDEN)
    return pe_rows.astype(F32), w1bd.astype(BF16), _group_block_diag(w2).astype(BF16)


def kernel(x, positions, mix_norm_g, w_in, q_norm_g, k_norm_g, cmp_pe, cmp_w1, cmp_w2, pool_w,
           pool_scale, w_pool_out, w_nsa_out, w_out, ffn_norm_g, w_up, conv_w, conv_b, w_down):
    B, S, _ = x.shape
    depth = w_in.shape[0]
    perm = _slab_perm()
    o_q = POOL_W
    o_kv = o_q + ATTN_W
    o_ng = o_kv + N_KV_SLOTS * KV_W
    o_mg = o_ng + N_HEADS * N_NSA_BRANCHES
    half = jnp.arange(ROT_HALF, dtype=F32)
    inv = ROPE_THETA ** (-(half * 2.0 / ROT_DIM))
    inv_row = jnp.tile(inv, LANES // ROT_HALF)[None, :]
    pos3 = positions[:, :, None]
    n_cmp_rows = S // CMP_STRIDE
    pos_cmp = positions[:, CMP_BLOCK - 1::CMP_STRIDE]
    pos_cmp = jnp.pad(pos_cmp, ((0, 0), (0, n_cmp_rows - pos_cmp.shape[1])))[:, :, None]
    n_sel = S // SEL_BLOCK
    n_cmp = (S - CMP_BLOCK) // CMP_STRIDE + 1
    sel_start = np.arange(n_sel)[:, None] * SEL_BLOCK
    cmp_start = np.arange(n_cmp)[None, :] * CMP_STRIDE
    ov = np.clip(np.minimum(sel_start + SEL_BLOCK, cmp_start + CMP_BLOCK)
                 - np.maximum(sel_start, cmp_start), 0, None) / CMP_BLOCK
    ovt = np.zeros((n_sel, n_cmp_rows), np.float32)
    ovt[:, :n_cmp] = ov
    ovt = jnp.asarray(ovt, BF16)

    for l in range(depth):
        w = w_in[l]
        wng = jnp.pad(w[:, o_ng:o_mg], ((0, 0), (0, LANES - (o_mg - o_ng))))
        q, kc_raw, vc_raw, ks, vs, kw, vw, gates, zp, gn = _in_proj(
            x, pos3, mix_norm_g[l][None, :],
            w[:, :o_q].astype(BF16), w[:, o_q:o_kv][:, perm].astype(BF16),
            w[:, o_kv:o_ng].astype(BF16), w[:, o_mg:].astype(BF16), wng.astype(BF16),
            inv_row, jnp.tile(q_norm_g[l], N_KV_GROUPS)[None, :],
            jnp.tile(k_norm_g[l], (1, N_KV_GROUPS)),
            pool_w[l].astype(BF16), pool_scale[l][None, :], w_pool_out[l].astype(BF16))
        pek, w1k, w2k = _compress_weights(cmp_pe[l, 0], cmp_w1[l, 0], cmp_w2[l, 0])
        pev, w1v, w2v = _compress_weights(cmp_pe[l, 1], cmp_w1[l, 1], cmp_w2[l, 1])
        attn = _nsa(q, kc_raw, vc_raw, ks, vs, kw, vw, gates, pos_cmp, pek, pev, w1k, w1v,
                    w2k, w2v, jnp.tile(k_norm_g[l, 0], N_KV_GROUPS)[None, :], inv_row, ovt)
        x = _ffn(x, zp, gn, attn, w_nsa_out[l][perm, :].astype(BF16), w_out[l].astype(BF16),
                 ffn_norm_g[l][None, :], w_up[l][:, :D_FF].astype(BF16),
                 w_up[l][:, D_FF:].astype(BF16), conv_w[l], conv_b[l][None, :],
                 w_down[l].astype(BF16))
    return x
```

```python
import functools

import numpy as np
import jax
import jax.numpy as jnp
from jax import lax
from jax.experimental import pallas as pl
from jax.experimental.pallas import tpu as pltpu

D_MODEL = 1024
POOL_WINDOWS = (2, 4, 8, 16)
POOL_GROUP_W = 128
POOL_W = len(POOL_WINDOWS) * POOL_GROUP_W
POOL_HALO = max(POOL_WINDOWS)
N_HEADS = 8
HEAD_DIM = 64
N_KV_GROUPS = 2
HEADS_PER_GROUP = N_HEADS // N_KV_GROUPS
ATTN_W = N_HEADS * HEAD_DIM
KV_W = N_KV_GROUPS * HEAD_DIM
N_KV_SLOTS = 6
N_NSA_BRANCHES = 3
CMP_BLOCK = 32
CMP_STRIDE = 16
CMP_HIDDEN = 128
SEL_BLOCK = 64
N_SELECT = 16
WINDOW = 512
ROPE_THETA = 500000.0
ROT_DIM = HEAD_DIM // 4
ROT_HALF = ROT_DIM // 2
D_FF = 2816
CONV_WIDTH = 3
RMS_EPS = 1e-6
NEG_INF = -1e30
SEL_FORCE = 1e9
LOG2_E = 1.4426950408889634

LANES = 128
SUBLANES = 8
VMEM_LIMIT = 56 * 1024 * 1024

TM_PROJ = 1024
TQ = 256
TK = 512
TM_FFN = 512
FF_CHUNK = 256

BF16 = jnp.bfloat16
F32 = jnp.float32


def _lane_iota(shape):
    return lax.broadcasted_iota(jnp.int32, shape, len(shape) - 1)


def _row_iota(shape):
    return lax.broadcasted_iota(jnp.int32, shape, len(shape) - 2)


def _dot(a, b):
    return jnp.dot(a, b, preferred_element_type=F32)


def _dot_nt(a, b):
    return lax.dot_general(a, b, (((1,), (1,)), ((), ())), preferred_element_type=F32)


def _half_norm(v, gain_row):
    lo = _lane_iota(v.shape) < HEAD_DIM
    sq = v * v
    ss_lo = jnp.sum(jnp.where(lo, sq, 0.0), axis=-1, keepdims=True)
    ss_hi = jnp.sum(jnp.where(lo, 0.0, sq), axis=-1, keepdims=True)
    ms = jnp.where(lo, ss_lo, ss_hi) * (1.0 / HEAD_DIM)
    return v * lax.rsqrt(ms + RMS_EPS) * gain_row


def _rope_factors(pos_col, inv_row):
    ang = pos_col.astype(F32) * inv_row
    d = _lane_iota(ang.shape) & (HEAD_DIM - 1)
    cos = jnp.where(d < ROT_DIM, jnp.cos(ang), 1.0)
    sin = jnp.sin(ang)
    sin = jnp.where(d < ROT_HALF, -sin, jnp.where(d < ROT_DIM, sin, 0.0))
    return cos, sin


def _rope(v, cos, sin):
    d = _lane_iota(v.shape) & (HEAD_DIM - 1)
    partner = jnp.where(d < ROT_HALF,
                        pltpu.roll(v, LANES - ROT_HALF, 1),
                        pltpu.roll(v, ROT_HALF, 1))
    return v * cos + partner * sin


def _in_proj_kernel(x_ref, pos_ref, g_ref, wu_ref, wq_ref, wkv_ref, wmg_ref, wng_ref,
                    inv_ref, qg_ref, kg_ref, pw_ref, ps_ref, wpo_ref,
                    q_out, kc_out, vc_out, ks_out, vs_out, kw_out, vw_out,
                    gate_out, zp_out, gn_out, ext_ref):
    si = pl.program_id(1)
    tm = x_ref.shape[0]

    @pl.when(si == 0)
    def _():
        ext_ref[0:POOL_HALO, :] = jnp.zeros((POOL_HALO, POOL_W), F32)

    x = x_ref[...]
    ms = jnp.mean(x * x, axis=-1, keepdims=True)
    hb = (x * lax.rsqrt(ms + RMS_EPS) * g_ref[...]).astype(BF16)

    cos, sin = _rope_factors(pos_ref[...], inv_ref[...])
    q = _dot(hb, wq_ref[...])
    scale = HEAD_DIM ** -0.5 * LOG2_E
    for j in range(ATTN_W // LANES):
        sl = slice(j * LANES, (j + 1) * LANES)
        qn = _rope(_half_norm(q[:, sl], qg_ref[...]), cos, sin)
        q_out[:, sl] = (qn * scale).astype(BF16)

    kv = _dot(hb, wkv_ref[...])
    kc_out[...] = kv[:, 0 * KV_W:1 * KV_W].astype(BF16)
    vc_out[...] = kv[:, 1 * KV_W:2 * KV_W].astype(BF16)
    ks = _half_norm(kv[:, 2 * KV_W:3 * KV_W], kg_ref[1:2, :])
    ks_out[:, :KV_W] = _rope(ks, cos, sin).astype(BF16)
    tok_blk = (si * tm + _row_iota((tm, LANES))) // SEL_BLOCK
    ks_out[:, KV_W:] = jnp.where(_lane_iota((tm, LANES)) == tok_blk, NEG_INF, 0.0).astype(BF16)
    v_sel = kv[:, 3 * KV_W:4 * KV_W]
    lo = _lane_iota(v_sel.shape) < HEAD_DIM
    vs_out[:, :KV_W] = jnp.where(lo, v_sel, 1.0).astype(BF16)
    vs_out[:, KV_W:] = jnp.where(lo, 1.0, v_sel).astype(BF16)
    kw = _half_norm(kv[:, 4 * KV_W:5 * KV_W], kg_ref[2:3, :])
    kw_out[...] = _rope(kw, cos, sin).astype(BF16)
    _store_value_ext(kv[:, 5 * KV_W:6 * KV_W], vw_out)

    u = _dot(hb, wu_ref[...])
    ext_ref[POOL_HALO:, :] = u
    t = si * tm + _row_iota((tm, 1))
    count = (t + 1).astype(F32)
    pooled = []
    for gi, w in enumerate(POOL_WINDOWS):
        sl = slice(gi * POOL_GROUP_W, (gi + 1) * POOL_GROUP_W)
        acc = ext_ref[:, sl]
        step = 1
        while step < w:
            acc = acc + pltpu.roll(acc, step, 0)
            step *= 2
        win_sum = acc[POOL_HALO:, :]
        mean = win_sum / jnp.minimum(count, float(w))
        pm = (mean - u[:, sl]).astype(BF16)
        pooled.append((_dot(pm, pw_ref[gi]) * ps_ref[:, sl]).astype(BF16))
    ext_ref[0:POOL_HALO, :] = ext_ref[tm:tm + POOL_HALO, :]
    y_pool = _dot(jnp.concatenate(pooled, axis=-1), wpo_ref[...])

    mg = _dot(hb, wmg_ref[...])
    zp_out[...] = (jax.nn.sigmoid(mg[:, :D_MODEL]) * y_pool).astype(BF16)
    gn_out[...] = jax.nn.sigmoid(mg[:, D_MODEL:]).astype(BF16)
    gate_out[...] = jax.nn.sigmoid(_dot(hb, wng_ref[...]))


def _store_value_ext(v, out):
    out[:, :KV_W] = v.astype(BF16)
    out[:, KV_W:] = jnp.ones(v.shape, BF16)


def _const_spec(shape):
    nd = len(shape)
    return pl.BlockSpec(shape, lambda b, s: (0,) * nd, pipeline_mode=pl.Buffered(1))


def _in_proj(x, pos3, g_row, wu, wq, wkv, wmg, wng, inv_row, qg_row, kg_rows, pw, ps_row, wpo):
    B, S, _ = x.shape
    tm = TM_PROJ
    tok = lambda w: pl.BlockSpec((None, tm, w), lambda b, s: (b, s, 0))
    out_shapes = (
        jax.ShapeDtypeStruct((B, S, ATTN_W), BF16),
        jax.ShapeDtypeStruct((B, S, KV_W), BF16),
        jax.ShapeDtypeStruct((B, S, KV_W), BF16),
        jax.ShapeDtypeStruct((B, S, 2 * KV_W), BF16),
        jax.ShapeDtypeStruct((B, S, 2 * KV_W), BF16),
        jax.ShapeDtypeStruct((B, S, KV_W), BF16),
        jax.ShapeDtypeStruct((B, S, 2 * KV_W), BF16),
        jax.ShapeDtypeStruct((B, S, LANES), F32),
        jax.ShapeDtypeStruct((B, S, D_MODEL), BF16),
        jax.ShapeDtypeStruct((B, S, D_MODEL), BF16),
    )
    out_specs = (tok(ATTN_W), tok(KV_W), tok(KV_W), tok(2 * KV_W), tok(2 * KV_W), tok(KV_W),
                 tok(2 * KV_W), tok(LANES), tok(D_MODEL), tok(D_MODEL))
    in_specs = [
        tok(D_MODEL),
        pl.BlockSpec((None, tm, 1), lambda b, s: (b, s, 0)),
        _const_spec(g_row.shape), _const_spec(wu.shape), _const_spec(wq.shape),
        _const_spec(wkv.shape), _const_spec(wmg.shape), _const_spec(wng.shape),
        _const_spec(inv_row.shape), _const_spec(qg_row.shape), _const_spec(kg_rows.shape),
        _const_spec(pw.shape), _const_spec(ps_row.shape), _const_spec(wpo.shape),
    ]
    return pl.pallas_call(
        _in_proj_kernel,
        grid=(B, S // tm),
        in_specs=in_specs,
        out_specs=out_specs,
        out_shape=out_shapes,
        scratch_shapes=[pltpu.VMEM((POOL_HALO + tm, POOL_W), F32)],
        compiler_params=pltpu.CompilerParams(
            dimension_semantics=("arbitrary", "arbitrary"),
            vmem_limit_bytes=VMEM_LIMIT),
        name="in_proj",
    )(x, pos3, g_row, wu, wq, wkv, wmg, wng, inv_row, qg_row, kg_rows, pw, ps_row, wpo)


def _compress(raw_ref, pe_ref, w1_ref, w2_ref):
    a = raw_ref[...].astype(F32)
    first = _dot((a + pe_ref[0:1, :]).astype(BF16), w1_ref[0])
    second = _dot((a + pe_ref[1:2, :]).astype(BF16), w1_ref[1])
    nblk = first.shape[0]
    hid = first + pltpu.roll(second, nblk - 1, 0)
    hid = hid * jax.nn.sigmoid(hid)
    return _dot(hid.astype(BF16), w2_ref[...])


def _gated(gates, branch, o):
    out = []
    for h in range(N_HEADS):
        c = N_NSA_BRANCHES * h + branch
        out.append(gates[:, c:c + 1] * o[h * TQ:(h + 1) * TQ])
    return out


def _group_halves(pv):
    grp_rows = pv.shape[0] // N_KV_GROUPS
    return jnp.concatenate([pv[g * grp_rows:(g + 1) * grp_rows, g * KV_W:(g + 1) * KV_W]
                            for g in range(N_KV_GROUPS)], axis=0)


def _nsa_kernel(q_ref, kcr_ref, vcr_ref, ks_ref, vs_ref, kw_ref, vw_ref, gate_ref, posc_ref,
                pek_ref, pev_ref, w1k_ref, w1v_ref, w2k_ref, w2v_ref, kg_ref, inv_ref, ovt_ref,
                o_ref, kc_s, vc_s, m_ref, acc_ref):
    qi = pl.program_id(1)

    @pl.when(qi == 0)
    def _():
        kc = _compress(kcr_ref, pek_ref, w1k_ref, w2k_ref)
        cos, sin = _rope_factors(posc_ref[...], inv_ref[...])
        kc_s[...] = _rope(_half_norm(kc, kg_ref[...]), cos, sin).astype(BF16)
        _store_value_ext(_compress(vcr_ref, pev_ref, w1v_ref, w2v_ref), vc_s)

    for i in range(ks_ref.shape[0] // TQ):
        pl.when(qi == i)(functools.partial(
            _nsa_tile, i * TQ, q_ref, ks_ref, vs_ref, kw_ref, vw_ref, gate_ref, ovt_ref,
            o_ref, kc_s, vc_s, m_ref, acc_ref))


def _nsa_tile(t0, q_ref, ks_ref, vs_ref, kw_ref, vw_ref, gate_ref, ovt_ref, o_ref,
              kc_s, vc_s, m_ref, acc_ref):
    seq = ks_ref.shape[0]
    grp_rows = HEADS_PER_GROUP * TQ
    t = t0 + _row_iota((TQ, 1))
    lane = _lane_iota((TQ, LANES))
    lo = lane < HEAD_DIM

    q = q_ref[...]
    zero = jnp.zeros((TQ, LANES), BF16)
    slabs = [q[:, j * LANES:(j + 1) * LANES] for j in range(HEADS_PER_GROUP)]
    qs = jnp.concatenate([jnp.where(lo, s, zero) for s in slabs]
                         + [jnp.where(lo, zero, s) for s in slabs], axis=0)
    rows = N_HEADS * TQ

    s = _dot_nt(qs, kc_s[...])
    vis = lane * CMP_STRIDE + (CMP_BLOCK - 1) <= t
    s = (s.reshape(N_HEADS, TQ, LANES) + jnp.where(vis, 0.0, NEG_INF)[None]).reshape(rows, LANES)
    p = jnp.exp2(s - jnp.max(s, axis=-1, keepdims=True))
    pv = _dot(p.astype(BF16), vc_s[...])
    any_visible = t >= CMP_BLOCK - 1
    inv_l = 1.0 / pv[:, KV_W:]
    gates = gate_ref[...]
    part = [jnp.where(any_visible, o, 0.0) for o in _gated(gates, 0, pv[:, :KV_W] * inv_l)]
    p = (p * inv_l).reshape(N_HEADS, TQ, LANES)
    p = jnp.where(any_visible[None], p, 0.0)

    n_win = WINDOW + TQ
    start = max(t0 - WINDOW, 0)
    dist = t - (start + _lane_iota((TQ, n_win)))
    bias = jnp.where((dist >= 0) & (dist < WINDOW), 0.0, NEG_INF)
    s = _dot_nt(qs, kw_ref[pl.ds(start, n_win), :])
    s = (s.reshape(N_HEADS, TQ, n_win) + bias[None]).reshape(rows, n_win)
    pw = jnp.exp2(s - jnp.max(s, axis=-1, keepdims=True)).astype(BF16)
    pv = _dot(pw, vw_ref[pl.ds(start, n_win), :])
    part = [a + b for a, b in zip(part, _gated(gates, 2, pv[:, :KV_W] / pv[:, KV_W:]))]

    n_sel = seq // SEL_BLOCK
    blk = _row_iota((n_sel, TQ))
    tq_lane = t0 + _lane_iota((n_sel, TQ))
    cur = tq_lane // SEL_BLOCK
    forced = (blk == 0) | (blk == cur) | (blk == cur - 1)
    future = blk > cur
    eye = jnp.where(_row_iota((TQ, TQ)) == _lane_iota((TQ, TQ)), 1.0, 0.0).astype(BF16)
    pad = jnp.zeros((LANES - n_sel, TQ), F32)
    q_wide = []
    for g in range(N_KV_GROUPS):
        psum = jnp.sum(p[g * HEADS_PER_GROUP:(g + 1) * HEADS_PER_GROUP], axis=0)
        ps_hi = psum.astype(BF16)
        ps_lo = (psum - ps_hi.astype(F32)).astype(BF16)
        imp = _dot_nt(ovt_ref[...], ps_hi) + _dot_nt(ovt_ref[...], ps_lo)
        val = jnp.where(forced, SEL_FORCE, jnp.where(future, NEG_INF, imp))
        rank = jnp.zeros((n_sel, TQ), jnp.int32)
        for d in range(1, n_sel):
            other = pltpu.roll(val, d, 0)
            ahead = (other > val) | ((other == val) & (blk >= d))
            rank = rank + ahead.astype(jnp.int32)
        dropped = jnp.concatenate([jnp.where(rank < N_SELECT, 0.0, 1.0), pad], axis=0)
        dropped = _dot_nt(eye, dropped.astype(BF16)).astype(BF16)
        r = slice(g * grp_rows, (g + 1) * grp_rows)
        q_wide.append(jnp.concatenate([qs[r], jnp.tile(dropped, (HEADS_PER_GROUP, 1))], axis=1))
    q_wide = jnp.concatenate(q_wide, axis=0)

    m_ref[...] = jnp.full(m_ref.shape, NEG_INF, F32)
    acc_ref[...] = jnp.zeros(acc_ref.shape, F32)

    def sel_tile(start, width, causal):
        s = _dot_nt(q_wide, ks_ref[start:start + width, :])
        if causal:
            bias = jnp.where(start + _lane_iota((TQ, width)) <= t, 0.0, NEG_INF)
            s = (s.reshape(N_HEADS, TQ, width) + bias[None]).reshape(rows, width)
        m_old = m_ref[...]
        m_new = jnp.maximum(m_old, jnp.max(s, axis=-1, keepdims=True))
        p = jnp.exp2(s - jnp.tile(m_new, (1, width // LANES))).astype(BF16)
        pv = _group_halves(_dot(p, vs_ref[start:start + width, :]))
        acc_ref[...] = jnp.exp2(m_old - m_new) * acc_ref[...] + pv
        m_ref[...] = m_new

    diag = (t0 // TK) * TK
    sel_tile(diag, t0 + TQ - diag, True)
    for start in range(0, diag, TK):
        sel_tile(start, TK, False)
    acc = acc_ref[...]
    o_sel = acc / pltpu.roll(acc, HEAD_DIM, 1)

    heads = [a + b for a, b in zip(part, _gated(gates, 1, o_sel))]
    for j in range(HEADS_PER_GROUP):
        o_ref[:, j * LANES:(j + 1) * LANES] = jnp.where(
            lo, heads[j], heads[HEADS_PER_GROUP + j]).astype(BF16)


def _nsa(q, kc_raw, vc_raw, ks, vs, kw, vw, gates, pos_cmp, pek, pev, w1k, w1v, w2k, w2v,
         kg_row, inv_row, ovt):
    B, S, _ = q.shape
    assert S % TK == 0 and S >= WINDOW + TQ and LANES >= S // SEL_BLOCK
    n_rows = S // CMP_STRIDE
    kv_spec = lambda w: pl.BlockSpec((None, S, w), lambda b, i: (b, 0, 0))
    raw_spec = pl.BlockSpec((None, n_rows, CMP_STRIDE * KV_W), lambda b, i: (b, 0, 0))
    consts = (pek, pev, w1k, w1v, w2k, w2v, kg_row, inv_row, ovt)
    in_specs = [
        pl.BlockSpec((None, TQ, ATTN_W), lambda b, i: (b, i, 0)),
        raw_spec, raw_spec, kv_spec(2 * KV_W), kv_spec(2 * KV_W), kv_spec(KV_W), kv_spec(2 * KV_W),
        pl.BlockSpec((None, TQ, LANES), lambda b, i: (b, i, 0)),
        pl.BlockSpec((None, n_rows, 1), lambda b, i: (b, 0, 0)),
    ] + [_const_spec(c.shape) for c in consts]
    rows = N_HEADS * TQ
    return pl.pallas_call(
        _nsa_kernel,
        grid=(B, S // TQ),
        in_specs=in_specs,
        out_specs=pl.BlockSpec((None, TQ, ATTN_W), lambda b, i: (b, i, 0)),
        out_shape=jax.ShapeDtypeStruct((B, S, ATTN_W), BF16),
        scratch_shapes=[
            pltpu.VMEM((n_rows, KV_W), BF16),
            pltpu.VMEM((n_rows, 2 * KV_W), BF16),
            pltpu.VMEM((rows, LANES), F32),
            pltpu.VMEM((rows, LANES), F32),
        ],
        compiler_params=pltpu.CompilerParams(
            dimension_semantics=("arbitrary", "arbitrary"),
            vmem_limit_bytes=VMEM_LIMIT),
        name="nsa",
    )(q, kc_raw.reshape(B, n_rows, CMP_STRIDE * KV_W), vc_raw.reshape(B, n_rows, CMP_STRIDE * KV_W),
      ks, vs, kw, vw, gates, pos_cmp, *consts)


def _ffn_kernel(x_ref, zp_ref, gn_ref, a_ref, wno_ref, wo_ref, g_ref, wug_ref, wuv_ref,
                cw_ref, cb_ref, wd_ref, o_ref, carry_ref, act_ref):
    si = pl.program_id(1)
    tm = x_ref.shape[0]

    @pl.when(si == 0)
    def _():
        carry_ref[...] = jnp.zeros(carry_ref.shape, F32)

    y_nsa = _dot(a_ref[...], wno_ref[...])
    merged = zp_ref[...].astype(F32) + gn_ref[...].astype(F32) * y_nsa
    x1 = x_ref[...] + _dot(merged.astype(BF16), wo_ref[...])
    ms = jnp.mean(x1 * x1, axis=-1, keepdims=True)
    hb = (x1 * lax.rsqrt(ms + RMS_EPS) * g_ref[...]).astype(BF16)

    row = _row_iota((tm, FF_CHUNK))
    for c in range(D_FF // FF_CHUNK):
        sl = slice(c * FF_CHUNK, (c + 1) * FF_CHUNK)
        gp = _dot(hb, wug_ref[:, sl])
        val = _dot(hb, wuv_ref[:, sl])
        prev = carry_ref[:, sl]
        p1 = prev[SUBLANES - 1:SUBLANES, :]
        p2 = prev[SUBLANES - 2:SUBLANES - 1, :]
        lag1 = jnp.where(row == 0, p1, pltpu.roll(gp, 1, 0))
        lag2 = jnp.where(row == 0, p2, jnp.where(row == 1, p1, pltpu.roll(gp, 2, 0)))
        carry_ref[:, sl] = gp[tm - SUBLANES:, :]
        gc = (cw_ref[0:1, sl] * lag2 + cw_ref[1:2, sl] * lag1 + cw_ref[2:3, sl] * gp
              + cb_ref[:, sl])
        act_ref[:, sl] = (gc * jax.nn.sigmoid(gc) * val).astype(BF16)
    o_ref[...] = x1 + _dot(act_ref[...], wd_ref[...])


def _ffn(x, zp, gn, attn, wno, wo, g_row, wug, wuv, cw, cb_row, wd):
    B, S, _ = x.shape
    tm = TM_FFN
    tok = lambda w: pl.BlockSpec((None, tm, w), lambda b, s: (b, s, 0))
    consts = (wno, wo, g_row, wug, wuv, cw, cb_row, wd)
    return pl.pallas_call(
        _ffn_kernel,
        grid=(B, S // tm),
        in_specs=[tok(D_MODEL), tok(D_MODEL), tok(D_MODEL), tok(ATTN_W)]
                 + [_const_spec(c.shape) for c in consts],
        out_specs=tok(D_MODEL),
        out_shape=jax.ShapeDtypeStruct((B, S, D_MODEL), F32),
        scratch_shapes=[pltpu.VMEM((SUBLANES, D_FF), F32),
                        pltpu.VMEM((tm, D_FF), BF16)],
        compiler_params=pltpu.CompilerParams(
            dimension_semantics=("arbitrary", "arbitrary"),
            vmem_limit_bytes=VMEM_LIMIT),
        name="merge_ffn",
    )(x, zp, gn, attn, *consts)


def _slab_perm():
    j, half, d = np.meshgrid(np.arange(HEADS_PER_GROUP), np.arange(N_KV_GROUPS),
                             np.arange(HEAD_DIM), indexing="ij")
    return ((j + HEADS_PER_GROUP * half) * HEAD_DIM + d).reshape(-1)


def _group_block_diag(w):
    z = jnp.zeros_like(w)
    return jnp.concatenate([jnp.concatenate([w, z], axis=-1),
                            jnp.concatenate([z, w], axis=-1)], axis=-2)


def _compress_weights(pe, w1, w2):
    half = CMP_BLOCK // 2
    pe_rows = jnp.tile(pe.reshape(2, half, 1, HEAD_DIM), (1, 1, N_KV_GROUPS, 1)).reshape(2, -1)
    w1h = w1.reshape(2, half, HEAD_DIM, CMP_HIDDEN)
    w1bd = _group_block_diag(w1h).reshape(2, half * KV_W, N_KV_GROUPS * CMP_HIDDEN)
    return pe_rows.astype(F32), w1bd.astype(BF16), _group_block_diag(w2).astype(BF16)


def kernel(x, positions, mix_norm_g, w_in, q_norm_g, k_norm_g, cmp_pe, cmp_w1, cmp_w2, pool_w,
           pool_scale, w_pool_out, w_nsa_out, w_out, ffn_norm_g, w_up, conv_w, conv_b, w_down):
    B, S, _ = x.shape
    depth = w_in.shape[0]
    perm = _slab_perm()
    o_q = POOL_W
    o_kv = o_q + ATTN_W
    o_ng = o_kv + N_KV_SLOTS * KV_W
    o_mg = o_ng + N_HEADS * N_NSA_BRANCHES
    half = jnp.arange(ROT_HALF, dtype=F32)
    inv = ROPE_THETA ** (-(half * 2.0 / ROT_DIM))
    inv_row = jnp.tile(inv, LANES // ROT_HALF)[None, :]
    pos3 = positions[:, :, None]
    n_cmp_rows = S // CMP_STRIDE
    pos_cmp = positions[:, CMP_BLOCK - 1::CMP_STRIDE]
    pos_cmp = jnp.pad(pos_cmp, ((0, 0), (0, n_cmp_rows - pos_cmp.shape[1])))[:, :, None]
    n_sel = S // SEL_BLOCK
    n_cmp = (S - CMP_BLOCK) // CMP_STRIDE + 1
    sel_start = np.arange(n_sel)[:, None] * SEL_BLOCK
    cmp_start = np.arange(n_cmp)[None, :] * CMP_STRIDE
    ov = np.clip(np.minimum(sel_start + SEL_BLOCK, cmp_start + CMP_BLOCK)
                 - np.maximum(sel_start, cmp_start), 0, None) / CMP_BLOCK
    ovt = np.zeros((n_sel, n_cmp_rows), np.float32)
    ovt[:, :n_cmp] = ov
    ovt = jnp.asarray(ovt, BF16)

    for l in range(depth):
        w = w_in[l]
        wng = jnp.pad(w[:, o_ng:o_mg], ((0, 0), (0, LANES - (o_mg - o_ng))))
        q, kc_raw, vc_raw, ks, vs, kw, vw, gates, zp, gn = _in_proj(
            x, pos3, mix_norm_g[l][None, :],
            w[:, :o_q].astype(BF16), w[:, o_q:o_kv][:, perm].astype(BF16),
            w[:, o_kv:o_ng].astype(BF16), w[:, o_mg:].astype(BF16), wng.astype(BF16),
            inv_row, jnp.tile(q_norm_g[l], N_KV_GROUPS)[None, :],
            jnp.tile(k_norm_g[l], (1, N_KV_GROUPS)),
            pool_w[l].astype(BF16), pool_scale[l][None, :], w_pool_out[l].astype(BF16))
        pek, w1k, w2k = _compress_weights(cmp_pe[l, 0], cmp_w1[l, 0], cmp_w2[l, 0])
        pev, w1v, w2v = _compress_weights(cmp_pe[l, 1], cmp_w1[l, 1], cmp_w2[l, 1])
        attn = _nsa(q, kc_raw, vc_raw, ks, vs, kw, vw, gates, pos_cmp, pek, pev, w1k, w1v,
                    w2k, w2v, jnp.tile(k_norm_g[l, 0], N_KV_GROUPS)[None, :], inv_row, ovt)
        x = _ffn(x, zp, gn, attn, w_nsa_out[l][perm, :].astype(BF16), w_out[l].astype(BF16),
                 ffn_norm_g[l][None, :], w_up[l][:, :D_FF].astype(BF16),
                 w_up[l][:, D_FF:].astype(BF16), conv_w[l], conv_b[l][None, :],
                 w_down[l].astype(BF16))
    return x
```
